```python
import jax, jax.numpy as jnp
from jax import lax
import numpy as np

D_MODEL = 4096
BATCH = 4
SEQ = 2048
DEPTH = 1
DEC_BATCH = 32
DEC_SEQ = 1
PAST_LEN = 8192
PAGE_SIZE = 128

HEAD_DIM = 128
H_A = 12
KVH_A = 4
H_IDX = 16
D_IDX = 64
TOPK_MAX = 256
H_B = 12
KVH_B = 4
H_C = 4
HD_C = 256
N_MEM = 256
ROPE_THETA = 500000.0
ROT_FRAC = 4
Q_BLOCK = 128
EPS = 1e-6

W_A = H_A * HEAD_DIM
W_B = H_B * HEAD_DIM
W_C = H_C * HD_C
IN_SIZES = (W_A, KVH_A * HEAD_DIM, KVH_A * HEAD_DIM, H_IDX * D_IDX, D_IDX, H_IDX, W_A,
            W_B, KVH_B * HEAD_DIM, KVH_B * HEAD_DIM, H_B, W_B,
            W_C, W_C, 3 * D_MODEL)
D_IN = sum(IN_SIZES)

kernel_name = 'hybrid_dsa_fox_memxattn_gated_step'


def _split_points():
    pts, acc = [], 0
    for size in IN_SIZES[:-1]:
        acc += size
        pts.append(acc)
    return pts


def _rmsnorm(x, g):
    xf = x.astype(jnp.float32)
    y = xf * lax.rsqrt(jnp.mean(xf * xf, axis=-1, keepdims=True) + EPS)
    return (y * g.astype(jnp.float32)).astype(x.dtype)


def _rotary(x, pos):
    rot = x.shape[-1] // ROT_FRAC
    half = rot // 2
    inv = jnp.power(jnp.float32(ROPE_THETA), -jnp.arange(half, dtype=jnp.float32) * (2.0 / rot))
    ang = pos.astype(jnp.float32)[:, None] * inv[None, :]
    cos = jnp.cos(ang)[None, :, None, :]
    sin = jnp.sin(ang)[None, :, None, :]
    xf = x.astype(jnp.float32)
    x1, x2, rest = xf[..., :half], xf[..., half:rot], xf[..., rot:]
    return jnp.concatenate([x1 * cos - x2 * sin, x1 * sin + x2 * cos, rest], axis=-1).astype(x.dtype)


def _project(x, pos, g_norm, w_in, b_forget):
    b, t, _ = x.shape
    xn = _rmsnorm(x, g_norm)
    (qa, ka, va, iq, ik, iw, za, qb, kb, vb, fb, zb, qc, zc, gates) = jnp.split(xn @ w_in, _split_points(), axis=-1)
    qa = _rotary(qa.reshape(b, t, H_A, HEAD_DIM), pos)
    ka = _rotary(ka.reshape(b, t, KVH_A, HEAD_DIM), pos)
    va = va.reshape(b, t, KVH_A, HEAD_DIM)
    iq = _rotary(iq.reshape(b, t, H_IDX, D_IDX), pos)
    ik = _rotary(ik.reshape(b, t, 1, D_IDX), pos)[:, :, 0]
    qb = qb.reshape(b, t, H_B, HEAD_DIM)
    kb = kb.reshape(b, t, KVH_B, HEAD_DIM)
    vb = vb.reshape(b, t, KVH_B, HEAD_DIM)
    logf = jax.nn.log_sigmoid(fb.astype(jnp.float32) + b_forget.astype(jnp.float32)).astype(x.dtype)
    qc = qc.reshape(b, t, H_C, HD_C)
    return qa, ka, va, iq, ik, iw, za, qb, kb, vb, logf, zb, qc, zc, gates


def _index_scores(iq, iw, ik, q_pos, k_pos):
    s = jnp.einsum('bthd,bsd->bths', iq, ik, preferred_element_type=jnp.float32) * (D_IDX ** -0.5)
    score = jnp.einsum('bths,bth->bts', jax.nn.relu(s), iw.astype(jnp.float32) * (H_IDX ** -0.5))
    return jnp.where(k_pos[None, None, :] <= q_pos[None, :, None], score, -jnp.inf)


def _take_rows(a, idx):
    return jax.vmap(lambda rows, i: rows[i])(a, idx)


def _gather_paged(pool, page_table, new_rows, idx):
    b, t, k = idx.shape
    in_past = idx < PAST_LEN
    ip = jnp.minimum(idx, PAST_LEN - 1)
    page = jnp.take_along_axis(page_table, (ip // PAGE_SIZE).reshape(b, t * k), axis=1).reshape(b, t, k)
    flat = pool.reshape((-1,) + pool.shape[2:])
    g_past = flat[page * PAGE_SIZE + ip % PAGE_SIZE]
    g_new = _take_rows(new_rows, jnp.clip(idx - PAST_LEN, 0, t - 1))
    mask = in_past.reshape(in_past.shape + (1,) * (g_past.ndim - 3))
    return jnp.where(mask, g_past, g_new)


def _sparse_attend(q, k_sel, v_sel, valid):
    b, t = q.shape[:2]
    qg = q.reshape(b, t, KVH_A, H_A // KVH_A, HEAD_DIM)
    s = jnp.einsum('btngd,btknd->btngk', qg, k_sel, preferred_element_type=jnp.float32) * (HEAD_DIM ** -0.5)
    s = jnp.where(valid[:, :, None, None, :], s, -jnp.inf)
    p = jax.nn.softmax(s, axis=-1).astype(v_sel.dtype)
    o = jnp.einsum('btngk,btknd->btngd', p, v_sel)
    return o.reshape(b, t, W_A)


def _fox_attend(q, k, v, c_q, c_k, q_pos, k_pos):
    b, t = q.shape[:2]
    l = k.shape[1]
    grp = H_B // KVH_B
    qg = q.reshape(b, t, KVH_B, grp, HEAD_DIM)
    s = jnp.einsum('btngd,bsnd->bngts', qg, k, preferred_element_type=jnp.float32) * (HEAD_DIM ** -0.5)
    cq = c_q.reshape(b, t, KVH_B, grp).transpose(0, 2, 3, 1)[..., None]
    ck = c_k.reshape(b, l, KVH_B, grp).transpose(0, 2, 3, 1)[..., None, :]
    s = jnp.where(k_pos[None, :] <= q_pos[:, None], s + cq - ck, -jnp.inf)
    p = jax.nn.softmax(s, axis=-1).astype(v.dtype)
    o = jnp.einsum('bngts,bsnd->btngd', p, v)
    return o.reshape(b, t, W_B)


def _cross_attend(q, mk, mv):
    b, t = q.shape[:2]
    s = jnp.einsum('bthd,bmhd->bhtm', q, mk, preferred_element_type=jnp.float32) * (HD_C ** -0.5)
    p = jax.nn.softmax(s, axis=-1).astype(mv.dtype)
    return jnp.einsum('bhtm,bmhd->bthd', p, mv).reshape(b, t, W_C)


def _mem_kv(mem, g_mem, w_mem_kv):
    b, m, _ = mem.shape
    mk, mv = jnp.split(_rmsnorm(mem, g_mem) @ w_mem_kv, 2, axis=-1)
    return mk.reshape(b, m, H_C, HD_C), mv.reshape(b, m, H_C, HD_C)


def _merge(x, o_a, za, o_b, zb, o_c, zc, gates, w_br_a, w_br_b, w_br_c, w_out):
    g_a, g_b, g_c = jnp.split(gates, 3, axis=-1)
    h = (jax.nn.sigmoid(g_a) * ((o_a * jax.nn.silu(za)) @ w_br_a)
         + jax.nn.sigmoid(g_b) * ((o_b * jax.nn.silu(zb)) @ w_br_b)
         + jax.nn.sigmoid(g_c) * ((o_c * jax.nn.silu(zc)) @ w_br_c))
    return x + h @ w_out


def _prompt_layer(x, mem, g_mem, w_mem_kv, g_norm, w_in, b_forget, w_br_a, w_br_b, w_br_c, w_out):
    b, s, _ = x.shape
    pos = jnp.arange(s, dtype=jnp.int32)
    (qa, ka, va, iq, ik, iw, za, qb, kb, vb, logf, zb, qc, zc, gates) = _project(x, pos, g_norm, w_in, b_forget)
    topk = min(TOPK_MAX, s // 4)
    nb = s // Q_BLOCK

    def blk(a):
        return a.reshape((b, nb, Q_BLOCK) + a.shape[2:]).swapaxes(0, 1)

    def unblk(a):
        return a.swapaxes(0, 1).reshape((b, s) + a.shape[3:])

    pos_b = pos.reshape(nb, Q_BLOCK)

    def dsa_block(args):
        q_t, iq_t, iw_t, pos_t = args
        scores = _index_scores(iq_t, iw_t, ik, pos_t, pos)
        _, idx = lax.top_k(scores, topk)
        return _sparse_attend(q_t, _take_rows(ka, idx), _take_rows(va, idx), idx <= pos_t[None, :, None])

    o_a = unblk(lax.map(dsa_block, (blk(qa), blk(iq), blk(iw), pos_b)))

    c = jnp.cumsum(logf.astype(jnp.float32), axis=1)

    def fox_block(args):
        q_t, cq_t, pos_t = args
        return _fox_attend(q_t, kb, vb, cq_t, c, pos_t, pos)

    o_b = unblk(lax.map(fox_block, (blk(qb), blk(c), pos_b)))

    mk, mv = _mem_kv(mem, g_mem, w_mem_kv)
    o_c = _cross_attend(qc, mk, mv)
    y = _merge(x, o_a, za, o_b, zb, o_c, zc, gates, w_br_a, w_br_b, w_br_c, w_out)
    return y, (ka, va, ik, kb, vb, logf, mk, mv)


def _sample_layer(x, ck_a, cv_a, c_idx, ck_b, cv_b, c_logf, cm_k, cm_v, page_table,
                  g_norm, w_in, b_forget, w_br_a, w_br_b, w_br_c, w_out):
    b, t, _ = x.shape
    length = PAST_LEN + t
    pos = PAST_LEN + jnp.arange(t, dtype=jnp.int32)
    k_pos = jnp.arange(length, dtype=jnp.int32)
    (qa, ka, va, iq, ik, iw, za, qb, kb, vb, logf, zb, qc, zc, gates) = _project(x, pos, g_norm, w_in, b_forget)

    def past_rows(pool):
        rows = pool[page_table]
        return rows.reshape((b, PAST_LEN) + pool.shape[2:])

    ik_all = jnp.concatenate([past_rows(c_idx), ik], axis=1)
    scores = _index_scores(iq, iw, ik_all, pos, k_pos)
    _, idx = lax.top_k(scores, min(TOPK_MAX, length // 4))
    o_a = _sparse_attend(qa, _gather_paged(ck_a, page_table, ka, idx),
                         _gather_paged(cv_a, page_table, va, idx), idx <= pos[None, :, None])

    kb_all = jnp.concatenate([past_rows(ck_b), kb], axis=1)
    vb_all = jnp.concatenate([past_rows(cv_b), vb], axis=1)
    logf_all = jnp.concatenate([past_rows(c_logf), logf], axis=1)
    c = jnp.cumsum(logf_all.astype(jnp.float32), axis=1)
    o_b = _fox_attend(qb, kb_all, vb_all, c[:, PAST_LEN:], c, pos, k_pos)

    o_c = _cross_attend(qc, cm_k, cm_v)
    y = _merge(x, o_a, za, o_b, zb, o_c, zc, gates, w_br_a, w_br_b, w_br_c, w_out)
    return y, (ka, va, ik, kb, vb, logf)


def setup_inputs(seed: int = 0) -> dict:
    key = jax.random.key(seed)
    ks = jax.random.split(key, 24)
    f32 = jnp.float32
    n_pages = PAST_LEN // PAGE_SIZE
    n_used = DEC_BATCH * n_pages
    n_pool = n_used + max(1, n_used // 4)

    def nrm(k, shape, scale=1.0):
        return scale * jax.random.normal(k, shape, f32)

    perm = jax.random.permutation(ks[10], n_pool)
    page_table = perm[:n_used].reshape(DEC_BATCH, n_pages).astype(jnp.int32)
    return {
        'x_prompt': nrm(ks[0], (BATCH, SEQ, D_MODEL)),
        'x_sample': nrm(ks[1], (DEC_BATCH, DEC_SEQ, D_MODEL)),
        'cache_a_k': nrm(ks[2], (DEPTH, n_pool, PAGE_SIZE, KVH_A, HEAD_DIM)),
        'cache_a_v': nrm(ks[3], (DEPTH, n_pool, PAGE_SIZE, KVH_A, HEAD_DIM)),
        'cache_a_idx': nrm(ks[4], (DEPTH, n_pool, PAGE_SIZE, D_IDX)),
        'cache_b_k': nrm(ks[5], (DEPTH, n_pool, PAGE_SIZE, KVH_B, HEAD_DIM)),
        'cache_b_v': nrm(ks[6], (DEPTH, n_pool, PAGE_SIZE, KVH_B, HEAD_DIM)),
        'cache_b_logf': jax.nn.log_sigmoid(2.0 + nrm(ks[7], (DEPTH, n_pool, PAGE_SIZE, H_B))),
        'cache_mem_k': nrm(ks[8], (DEPTH, DEC_BATCH, N_MEM, H_C, HD_C)),
        'cache_mem_v': nrm(ks[9], (DEPTH, DEC_BATCH, N_MEM, H_C, HD_C)),
        'page_table': page_table,
        'mem_prompt': nrm(ks[11], (BATCH, N_MEM, D_MODEL)),
        'g_norm': 1.0 + nrm(ks[12], (DEPTH, D_MODEL), 0.02),
        'w_in': nrm(ks[13], (DEPTH, D_MODEL, D_IN), D_MODEL ** -0.5),
        'b_forget': 2.0 + nrm(ks[14], (DEPTH, H_B), 0.5),
        'w_br_a': nrm(ks[15], (DEPTH, W_A, D_MODEL), W_A ** -0.5),
        'w_br_b': nrm(ks[16], (DEPTH, W_B, D_MODEL), W_B ** -0.5),
        'w_br_c': nrm(ks[17], (DEPTH, W_C, D_MODEL), W_C ** -0.5),
        'w_out': nrm(ks[18], (DEPTH, D_MODEL, D_MODEL), D_MODEL ** -0.5),
        'g_mem': 1.0 + nrm(ks[19], (DEPTH, D_MODEL), 0.02),
        'w_mem_kv': nrm(ks[20], (DEPTH, D_MODEL, 2 * W_C), D_MODEL ** -0.5),
        'g_final': 1.0 + nrm(ks[21], (D_MODEL,), 0.02),
    }


def reference(x_prompt, x_sample, cache_a_k, cache_a_v, cache_a_idx, cache_b_k, cache_b_v, cache_b_logf,
              cache_mem_k, cache_mem_v, page_table, mem_prompt, g_norm, w_in, b_forget, w_br_a, w_br_b,
              w_br_c, w_out, g_mem, w_mem_kv, g_final):
    y_p, y_s = x_prompt, x_sample
    p_states, s_states = [], []
    for l in range(DEPTH):
        y_p, st_p = _prompt_layer(y_p, mem_prompt, g_mem[l], w_mem_kv[l], g_norm[l], w_in[l], b_forget[l],
                                  w_br_a[l], w_br_b[l], w_br_c[l], w_out[l])
        y_s, st_s = _sample_layer(y_s, cache_a_k[l], cache_a_v[l], cache_a_idx[l], cache_b_k[l], cache_b_v[l],
                                  cache_b_logf[l], cache_mem_k[l], cache_mem_v[l], page_table,
                                  g_norm[l], w_in[l], b_forget[l], w_br_a[l], w_br_b[l], w_br_c[l], w_out[l])
        p_states.append(st_p)
        s_states.append(st_s)
    p_ak, p_av, p_ai, p_bk, p_bv, p_bf, p_mk, p_mv = [jnp.stack(t) for t in zip(*p_states)]
    s_ak, s_av, s_ai, s_bk, s_bv, s_bf = [jnp.stack(t) for t in zip(*s_states)]
    y_prompt = _rmsnorm(y_p, g_final)
    y_sample = _rmsnorm(y_s, g_final)
    return (y_prompt, y_sample, p_ak, p_av, p_ai, p_bk, p_bv, p_bf, p_mk, p_mv,
            s_ak, s_av, s_ai, s_bk, s_bv, s_bf)
```

```python
import functools

import numpy as np
import jax
import jax.numpy as jnp
from jax import lax
from jax.experimental import pallas as pl
from jax.experimental.pallas import tpu as pltpu

D_MODEL = 4096
BATCH = 4
SEQ = 2048
DEC_BATCH = 32
PAST_LEN = 8192
PAGE_SIZE = 128
HEAD_DIM = 128
H_A = 12
KVH_A = 4
H_IDX = 16
D_IDX = 64
TOPK_MAX = 256
H_B = 12
KVH_B = 4
H_C = 4
HD_C = 256
N_MEM = 256
ROPE_THETA = 500000.0
ROT_FRAC = 4
EPS = 1e-6

F32 = jnp.float32
BF16 = jnp.bfloat16
NEG = -1e30
LANES = 128
HEAD_ROWS = 16
MIB = 1024 * 1024

W_A = H_A * HEAD_DIM
W_B = H_B * HEAD_DIM
W_C = H_C * HD_C
W_KV = KVH_A * HEAD_DIM
SMALL_W = 512
LF_LANE = D_IDX + H_IDX


def _layout():
    names = ("qa", "qb", "za", "zb", "ka", "va", "kb", "vb", "iq", "qc", "zc", "gates", "small")
    sizes = (W_A, W_B, W_A, W_B, W_KV, W_KV, W_KV, W_KV, H_IDX * D_IDX, W_C, W_C, 3 * D_MODEL, SMALL_W)
    off, acc = {}, 0
    for n, s in zip(names, sizes):
        off[n] = acc
        acc += s
    return off, acc


def _params(n_axes, vmem_mib):
    return pltpu.CompilerParams(dimension_semantics=("arbitrary",) * n_axes,
                                vmem_limit_bytes=vmem_mib * MIB)


def _sigmoid(z):
    return 1.0 / (1.0 + jnp.exp(-z))


def _silu(z):
    return z * _sigmoid(z)


def _log_sigmoid(z):
    return -(jnp.maximum(-z, 0.0) + jnp.log1p(jnp.exp(-jnp.abs(z))))


def _rot(x, c, sa, sb, half):
    n = x.shape[-1]
    return x * c + pltpu.roll(x, n - half, 1) * sa + pltpu.roll(x, half, 1) * sb


def _dot_nt(a, b):
    return lax.dot_general(a, b, (((1,), (1,)), ((), ())), preferred_element_type=F32)


def _split3(x):
    hi = x.astype(BF16)
    r = x - hi.astype(F32)
    mid = r.astype(BF16)
    lo = (r - mid.astype(F32)).astype(BF16)
    return hi, mid, lo


def _kth_largest_key(key, k):
    kf = jnp.float32(k)

    def count_ge(t):
        return jnp.sum((key >= t).astype(F32), axis=1, keepdims=True)

    int_min = jnp.int32(-2 ** 31)
    thr = jnp.where(count_ge(jnp.int32(0)) >= kf, jnp.int32(0), int_min)
    thr = jnp.broadcast_to(thr, (key.shape[0], 1)).astype(jnp.int32)

    def body(i, thr):
        cand = thr + jnp.left_shift(jnp.int32(1), jnp.int32(30) - i)
        return jnp.where(count_ge(cand) >= kf, cand, thr)

    return lax.fori_loop(0, 31, body, thr)


def _sortable_key(scores):
    bits = lax.bitcast_convert_type(scores + 0.0, jnp.int32)
    return jnp.where(bits < 0, bits ^ jnp.int32(0x7FFFFFFF), bits)


def _rmsnorm_body(x_ref, g_ref, o_ref):
    x = x_ref[...]
    ms = jnp.mean(x * x, axis=-1, keepdims=True)
    o_ref[...] = ((x * lax.rsqrt(ms + EPS)) * g_ref[...]).astype(o_ref.dtype)


def _rmsnorm(x, g, out_dtype, bm):
    m, d = x.shape
    bm = min(bm, m)
    return pl.pallas_call(
        _rmsnorm_body,
        grid=(m // bm,),
        in_specs=[pl.BlockSpec((bm, d), lambda i: (i, 0)), pl.BlockSpec((1, d), lambda i: (0, 0))],
        out_specs=pl.BlockSpec((bm, d), lambda i: (i, 0)),
        out_shape=jax.ShapeDtypeStruct((m, d), out_dtype),
        compiler_params=_params(1, 40),
        name="rmsnorm",
    )(x, g.reshape(1, d))


def _mm_body(x_ref, w_ref, o_ref):
    o_ref[...] = jnp.dot(x_ref[...], w_ref[...], preferred_element_type=F32)


def _matmul(x, w, bm, bn, name):
    m, k = x.shape
    n = w.shape[1]
    bm, bn = min(bm, m), min(bn, n)
    return pl.pallas_call(
        _mm_body,
        grid=(m // bm, n // bn),
        in_specs=[pl.BlockSpec((bm, k), lambda i, j: (i, 0)), pl.BlockSpec((k, bn), lambda i, j: (0, j))],
        out_specs=pl.BlockSpec((bm, bn), lambda i, j: (i, j)),
        out_shape=jax.ShapeDtypeStruct((m, n), F32),
        compiler_params=_params(2, 48),
        name=name,
    )(x, w)


def _post_body(ka_ref, sm_ref, tab_ref, bf_ref, ka_o, sm_o):
    c, sa, sb = tab_ref[0], tab_ref[1], tab_ref[2]
    for h in range(KVH_A):
        sl = slice(h * HEAD_DIM, (h + 1) * HEAD_DIM)
        ka_o[:, sl] = _rot(ka_ref[:, sl], c, sa, sb, HEAD_DIM // ROT_FRAC // 2)
    x = sm_ref[...]
    r = _rot(x, tab_ref[6], tab_ref[7], tab_ref[8], D_IDX // ROT_FRAC // 2)
    lf = _log_sigmoid(x + bf_ref[...])
    lane = lax.broadcasted_iota(jnp.int32, x.shape, 1)
    sm_o[...] = jnp.where((lane >= LF_LANE) & (lane < LF_LANE + H_B), lf, r)


def _post(proj, tab, bf_row, seq_rows, bm):
    off, _ = _layout()
    m = proj.shape[0]
    bm = min(bm, m)
    tb = seq_rows // bm
    return pl.pallas_call(
        _post_body,
        grid=(m // bm,),
        in_specs=[pl.BlockSpec((bm, W_KV), lambda i: (i, off["ka"] // W_KV)),
                  pl.BlockSpec((bm, LANES), lambda i: (i, off["small"] // LANES)),
                  pl.BlockSpec((9, bm, LANES), lambda i: (0, i % tb, 0)),
                  pl.BlockSpec((1, LANES), lambda i: (0, 0))],
        out_specs=[pl.BlockSpec((bm, W_KV), lambda i: (i, 0)), pl.BlockSpec((bm, LANES), lambda i: (i, 0))],
        out_shape=[jax.ShapeDtypeStruct((m, W_KV), F32), jax.ShapeDtypeStruct((m, LANES), F32)],
        compiler_params=_params(1, 40),
        name="kv_post",
    )(proj, proj, tab, bf_row)


def _qrot_body(qa_ref, iq_ref, tab_ref, qa_o, iq_o):
    for h in range(H_A):
        sl = slice(h * HEAD_DIM, (h + 1) * HEAD_DIM)
        qa_o[:, sl] = _rot(qa_ref[:, sl], tab_ref[0], tab_ref[1], tab_ref[2], HEAD_DIM // ROT_FRAC // 2)
    for hp in range(H_IDX * D_IDX // LANES):
        sl = slice(hp * LANES, (hp + 1) * LANES)
        iq_o[:, sl] = _rot(iq_ref[:, sl], tab_ref[3], tab_ref[4], tab_ref[5], D_IDX // ROT_FRAC // 2)


def _qrot(proj, tab):
    off, _ = _layout()
    m = proj.shape[0]
    wi = H_IDX * D_IDX
    return pl.pallas_call(
        _qrot_body,
        grid=(1,),
        in_specs=[pl.BlockSpec((m, W_A), lambda i: (0, off["qa"] // W_A)),
                  pl.BlockSpec((m, wi), lambda i: (0, off["iq"] // wi)),
                  pl.BlockSpec((9, m, LANES), lambda i: (0, 0, 0))],
        out_specs=[pl.BlockSpec((m, W_A), lambda i: (0, 0)), pl.BlockSpec((m, wi), lambda i: (0, 0))],
        out_shape=[jax.ShapeDtypeStruct((m, W_A), F32), jax.ShapeDtypeStruct((m, wi), F32)],
        compiler_params=_params(1, 32),
        name="q_rotary",
    )(proj, proj, tab)


def _cumsum_body(sm_ref, ccol_ref, crow_ref):
    r = lax.broadcasted_iota(jnp.int32, (LANES, LANES), 0)
    c = lax.broadcasted_iota(jnp.int32, (LANES, LANES), 1)
    tri = (c <= r).astype(BF16)
    keep = (c >= LF_LANE) & (c < LF_LANE + H_B)
    carry = jnp.zeros((1, LANES), F32)
    for blk in range(ccol_ref.shape[0] // LANES):
        rows = slice(blk * LANES, (blk + 1) * LANES)
        x = jnp.where(keep, sm_ref[rows, :], 0.0)
        hi, mid, lo = _split3(x)
        cb = (jnp.dot(tri, hi, preferred_element_type=F32) + jnp.dot(tri, mid, preferred_element_type=F32)
              + jnp.dot(tri, lo, preferred_element_type=F32)) + carry
        carry = cb[LANES - 1:LANES, :]
        ccol_ref[rows, :] = cb
        crow_ref[0, :, rows] = cb.T


def _cumsum(small, batch, seq):
    return pl.pallas_call(
        _cumsum_body,
        grid=(batch,),
        in_specs=[pl.BlockSpec((seq, LANES), lambda b: (b, 0))],
        out_specs=[pl.BlockSpec((seq, LANES), lambda b: (b, 0)), pl.BlockSpec((1, LANES, seq), lambda b: (b, 0, 0))],
        out_shape=[jax.ShapeDtypeStruct((batch * seq, LANES), F32), jax.ShapeDtypeStruct((batch, LANES, seq), F32)],
        compiler_params=_params(1, 32),
        name="fox_cumsum",
    )(small)


def _softmax_av(s, v):
    m = jnp.max(s, axis=1, keepdims=True)
    p = jnp.exp(s - m)
    l = jnp.sum(p, axis=1, keepdims=True)
    return jnp.dot(p.astype(BF16), v, preferred_element_type=F32) / l


def _dsa_prompt_body(topk, q_ref, iq_ref, sm_ref, z_ref, k_ref, v_ref, ik_ref, tab_ref, u_ref):
    tq = q_ref.shape[0]
    s_len = k_ref.shape[0]
    row0 = pl.program_id(1) * tq
    rows = row0 + lax.broadcasted_iota(jnp.int32, (tq, s_len), 0)
    cols = lax.broadcasted_iota(jnp.int32, (tq, s_len), 1)
    causal = cols <= rows

    acc = jnp.zeros((tq, s_len), F32)
    for hp in range(H_IDX * D_IDX // LANES):
        x = iq_ref[:, hp * LANES:(hp + 1) * LANES]
        xb = _rot(x, tab_ref[3], tab_ref[4], tab_ref[5], D_IDX // ROT_FRAC // 2).astype(BF16)
        for j in range(LANES // D_IDX):
            h = hp * (LANES // D_IDX) + j
            s = _dot_nt(xb, ik_ref[:, j * LANES:(j + 1) * LANES]) * (D_IDX ** -0.5)
            w = sm_ref[:, D_IDX + h:D_IDX + h + 1] * (H_IDX ** -0.5)
            acc = acc + jnp.maximum(s, 0.0) * w
    key = _sortable_key(jnp.where(causal, acc, -jnp.inf))
    sel = (key >= _kth_largest_key(key, topk)) & causal
    bias = jnp.where(sel, 0.0, NEG)

    grp = H_A // KVH_A
    for h in range(H_A):
        sl = slice(h * HEAD_DIM, (h + 1) * HEAD_DIM)
        g = h // grp
        gs = slice(g * HEAD_DIM, (g + 1) * HEAD_DIM)
        qh = _rot(q_ref[:, sl], tab_ref[0], tab_ref[1], tab_ref[2], HEAD_DIM // ROT_FRAC // 2).astype(BF16)
        s = _dot_nt(qh, k_ref[:, gs]) * (HEAD_DIM ** -0.5) + bias
        o = _softmax_av(s, v_ref[:, gs])
        u_ref[:, sl] = (o * _silu(z_ref[:, sl])).astype(u_ref.dtype)


def _dsa_prompt(proj, small, ka_bf, va_bf, ik_bf, tab, batch, seq, tq):
    off, _ = _layout()
    nq = seq // tq
    wi = H_IDX * D_IDX
    topk = min(TOPK_MAX, seq // 4)
    row = lambda b, q: b * nq + q
    return pl.pallas_call(
        functools.partial(_dsa_prompt_body, topk),
        grid=(batch, nq),
        in_specs=[pl.BlockSpec((tq, W_A), lambda b, q: (row(b, q), off["qa"] // W_A)),
                  pl.BlockSpec((tq, wi), lambda b, q: (row(b, q), off["iq"] // wi)),
                  pl.BlockSpec((tq, LANES), lambda b, q: (row(b, q), 0)),
                  pl.BlockSpec((tq, W_A), lambda b, q: (row(b, q), off["za"] // W_A)),
                  pl.BlockSpec((seq, W_KV), lambda b, q: (b, 0)),
                  pl.BlockSpec((seq, W_KV), lambda b, q: (b, 0)),
                  pl.BlockSpec((seq, 2 * LANES), lambda b, q: (b, 0)),
                  pl.BlockSpec((9, tq, LANES), lambda b, q: (0, q, 0))],
        out_specs=pl.BlockSpec((tq, W_A), lambda b, q: (row(b, q), 0)),
        out_shape=jax.ShapeDtypeStruct((batch * seq, W_A), BF16),
        compiler_params=_params(2, 56),
        name="dsa_prompt",
    )(proj, proj, small, proj, ka_bf, va_bf, ik_bf, tab)


def _fox_prompt_body(q_ref, z_ref, k_ref, v_ref, ccol_ref, crow_ref, u_ref):
    tq = q_ref.shape[0]
    s_len = k_ref.shape[0]
    row0 = pl.program_id(1) * tq
    rows = row0 + lax.broadcasted_iota(jnp.int32, (tq, s_len), 0)
    cols = lax.broadcasted_iota(jnp.int32, (tq, s_len), 1)
    bias = jnp.where(cols <= rows, 0.0, NEG)
    grp = H_B // KVH_B
    for h in range(H_B):
        sl = slice(h * HEAD_DIM, (h + 1) * HEAD_DIM)
        g = h // grp
        gs = slice(g * HEAD_DIM, (g + 1) * HEAD_DIM)
        s = _dot_nt(q_ref[:, sl].astype(BF16), k_ref[:, gs]) * (HEAD_DIM ** -0.5)
        s = s + ccol_ref[:, LF_LANE + h:LF_LANE + h + 1] - crow_ref[0, LF_LANE + h:LF_LANE + h + 1, :]
        o = _softmax_av(s + bias, v_ref[:, gs])
        u_ref[:, sl] = (o * _silu(z_ref[:, sl])).astype(u_ref.dtype)


def _fox_prompt(proj, kb_bf, vb_bf, ccol, crow, batch, seq, tq):
    off, _ = _layout()
    nq = seq // tq
    row = lambda b, q: b * nq + q
    return pl.pallas_call(
        _fox_prompt_body,
        grid=(batch, nq),
        in_specs=[pl.BlockSpec((tq, W_B), lambda b, q: (row(b, q), off["qb"] // W_B)),
                  pl.BlockSpec((tq, W_B), lambda b, q: (row(b, q), off["zb"] // W_B)),
                  pl.BlockSpec((seq, W_KV), lambda b, q: (b, 0)),
                  pl.BlockSpec((seq, W_KV), lambda b, q: (b, 0)),
                  pl.BlockSpec((tq, LANES), lambda b, q: (row(b, q), 0)),
                  pl.BlockSpec((1, LANES, seq), lambda b, q: (b, 0, 0))],
        out_specs=pl.BlockSpec((tq, W_B), lambda b, q: (row(b, q), 0)),
        out_shape=jax.ShapeDtypeStruct((batch * seq, W_B), BF16),
        compiler_params=_params(2, 56),
        name="fox_prompt",
    )(proj, proj, kb_bf, vb_bf, ccol, crow)


def _cross_prompt_body(q_ref, z_ref, k_ref, v_ref, u_ref):
    for h in range(H_C):
        sl = slice(h * HD_C, (h + 1) * HD_C)
        s = _dot_nt(q_ref[:, sl].astype(BF16), k_ref[:, sl]) * (HD_C ** -0.5)
        o = _softmax_av(s, v_ref[:, sl])
        u_ref[:, sl] = (o * _silu(z_ref[:, sl])).astype(u_ref.dtype)


def _cross_prompt(proj, mk_bf, mv_bf, batch, seq, tq):
    off, _ = _layout()
    nq = seq // tq
    row = lambda b, q: b * nq + q
    return pl.pallas_call(
        _cross_prompt_body,
        grid=(batch, nq),
        in_specs=[pl.BlockSpec((tq, W_C), lambda b, q: (row(b, q), off["qc"] // W_C)),
                  pl.BlockSpec((tq, W_C), lambda b, q: (row(b, q), off["zc"] // W_C)),
                  pl.BlockSpec((N_MEM, W_C), lambda b, q: (b, 0)),
                  pl.BlockSpec((N_MEM, W_C), lambda b, q: (b, 0))],
        out_specs=pl.BlockSpec((tq, W_C), lambda b, q: (row(b, q), 0)),
        out_shape=jax.ShapeDtypeStruct((batch * seq, W_C), BF16),
        compiler_params=_params(2, 40),
        name="cross_prompt",
    )(proj, proj, mk_bf, mv_bf)


def _merge_body(ua_ref, ub_ref, uc_ref, wa_ref, wb_ref, wc_ref, ga_ref, gb_ref, gc_ref, h_ref):
    h = (_sigmoid(ga_ref[...]) * jnp.dot(ua_ref[...], wa_ref[...], preferred_element_type=F32)
         + _sigmoid(gb_ref[...]) * jnp.dot(ub_ref[...], wb_ref[...], preferred_element_type=F32)
         + _sigmoid(gc_ref[...]) * jnp.dot(uc_ref[...], wc_ref[...], preferred_element_type=F32))
    h_ref[...] = h.astype(h_ref.dtype)


def _merge(ua, ub, uc, wa, wb, wc, proj, bm, bn):
    off, _ = _layout()
    m = ua.shape[0]
    d = wa.shape[1]
    bm, bn = min(bm, m), min(bn, d)
    g0 = off["gates"] // bn
    nd = d // bn
    return pl.pallas_call(
        _merge_body,
        grid=(m // bm, nd),
        in_specs=[pl.BlockSpec((bm, W_A), lambda i, j: (i, 0)),
                  pl.BlockSpec((bm, W_B), lambda i, j: (i, 0)),
                  pl.BlockSpec((bm, W_C), lambda i, j: (i, 0)),
                  pl.BlockSpec((W_A, bn), lambda i, j: (0, j)),
                  pl.BlockSpec((W_B, bn), lambda i, j: (0, j)),
                  pl.BlockSpec((W_C, bn), lambda i, j: (0, j)),
                  pl.BlockSpec((bm, bn), lambda i, j: (i, g0 + j)),
                  pl.BlockSpec((bm, bn), lambda i, j: (i, g0 + nd + j)),
                  pl.BlockSpec((bm, bn), lambda i, j: (i, g0 + 2 * nd + j))],
        out_specs=pl.BlockSpec((bm, bn), lambda i, j: (i, j)),
        out_shape=jax.ShapeDtypeStruct((m, d), BF16),
        compiler_params=_params(2, 48),
        name="merge",
    )(ua, ub, uc, wa, wb, wc, proj, proj, proj)


def _outproj_body(x_ref, h_ref, w_ref, y_ref):
    y_ref[...] = x_ref[...] + jnp.dot(h_ref[...], w_ref[...], preferred_element_type=F32)


def _outproj(x, h, w, bm, bn):
    m, d = x.shape
    bm, bn = min(bm, m), min(bn, d)
    return pl.pallas_call(
        _outproj_body,
        grid=(m // bm, d // bn),
        in_specs=[pl.BlockSpec((bm, bn), lambda i, j: (i, j)),
                  pl.BlockSpec((bm, d), lambda i, j: (i, 0)),
                  pl.BlockSpec((d, bn), lambda i, j: (0, j))],
        out_specs=pl.BlockSpec((bm, bn), lambda i, j: (i, j)),
        out_shape=jax.ShapeDtypeStruct((m, d), F32),
        compiler_params=_params(2, 48),
        name="out_proj",
    )(x, h, w)


def _idx_sample_body(pps, pt_ref, iq_ref, w_ref, ikn_ref, *rest):
    pages = rest[:pps]
    sc_ref, own_ref = rest[pps], rest[pps + 1]
    iq = iq_ref[0]
    w = w_ref[0] * (H_IDX ** -0.5)
    for i in range(pps):
        s = _dot_nt(iq, pages[i][0].astype(BF16)) * (D_IDX ** -0.5)
        sc_ref[0, :, i * PAGE_SIZE:(i + 1) * PAGE_SIZE] = jnp.sum(jnp.maximum(s, 0.0) * w, axis=0, keepdims=True)

    @pl.when(pl.program_id(1) == 0)
    def _():
        kn = ikn_ref[0].astype(F32)
        s = jnp.sum(iq.astype(F32) * kn, axis=1, keepdims=True) * (D_IDX ** -0.5)
        own = jnp.sum(jnp.maximum(s, 0.0) * w, axis=0, keepdims=True)
        lane = lax.broadcasted_iota(jnp.int32, (1, LANES), 1)
        own_ref[0] = jnp.where(lane == 0, own, -jnp.inf)


def _idx_sample(page_table, iq_bf, iw, ik_new_bf, pool_idx, pps):
    b, n_pages = page_table.shape
    npc = n_pages // pps
    page_specs = [pl.BlockSpec((1, PAGE_SIZE, D_IDX), lambda bi, pc, pt, i=i: (pt[bi, pc * pps + i], 0, 0))
                  for i in range(pps)]
    grid_spec = pltpu.PrefetchScalarGridSpec(
        num_scalar_prefetch=1,
        grid=(b, npc),
        in_specs=[pl.BlockSpec((1, H_IDX, D_IDX), lambda bi, pc, pt: (bi, 0, 0)),
                  pl.BlockSpec((1, H_IDX, 1), lambda bi, pc, pt: (bi, 0, 0)),
                  pl.BlockSpec((1, 1, D_IDX), lambda bi, pc, pt: (bi, 0, 0))] + page_specs,
        out_specs=[pl.BlockSpec((1, 1, pps * PAGE_SIZE), lambda bi, pc, pt: (bi, 0, pc)),
                   pl.BlockSpec((1, 1, LANES), lambda bi, pc, pt: (bi, 0, 0))],
    )
    return pl.pallas_call(
        functools.partial(_idx_sample_body, pps),
        grid_spec=grid_spec,
        out_shape=[jax.ShapeDtypeStruct((b, 1, n_pages * PAGE_SIZE), F32),
                   jax.ShapeDtypeStruct((b, 1, LANES), F32)],
        compiler_params=_params(2, 32),
        name="idx_sample",
    )(page_table, iq_bf, iw, ik_new_bf, *([pool_idx] * pps))


def _topk_sample_body(topk, n_valid, sc_ref, bias_ref):
    sc = sc_ref[...]
    lane = lax.broadcasted_iota(jnp.int32, sc.shape, 1)
    valid = lane < n_valid
    key = _sortable_key(jnp.where(valid, sc, -jnp.inf))
    sel = (key >= _kth_largest_key(key, topk)) & valid
    bias_ref[...] = jnp.where(sel, 0.0, NEG)


def _topk_sample(scores, n_valid):
    topk = min(TOPK_MAX, n_valid // 4)
    return pl.pallas_call(
        functools.partial(_topk_sample_body, topk, n_valid),
        grid=(1,),
        in_specs=[pl.BlockSpec(scores.shape, lambda i: (0, 0))],
        out_specs=pl.BlockSpec(scores.shape, lambda i: (0, 0)),
        out_shape=jax.ShapeDtypeStruct(scores.shape, F32),
        compiler_params=_params(1, 32),
        name="topk_sample",
    )(scores)


def _fox_bias_body(pps, pt_ref, lfn_ref, *rest):
    pages = rest[:pps]
    d_ref, x_ref, carry_ref = rest[pps], rest[pps + 1], rest[pps + 2]
    r = lax.broadcasted_iota(jnp.int32, (LANES, LANES), 0)
    c = lax.broadcasted_iota(jnp.int32, (LANES, LANES), 1)
    upper = (r > c).astype(BF16)

    @pl.when(pl.program_id(1) == 0)
    def _():
        x_ref[...] = jnp.zeros_like(x_ref)
        x_ref[0:1, :] = lfn_ref[0]
        carry_ref[...] = x_ref[...].T[0:HEAD_ROWS, 0:1]
        x_ref[0:1, :] = jnp.zeros((1, LANES), F32)

    for i in reversed(range(pps)):
        x_ref[:, 0:H_B] = pages[i][0]
        xt = x_ref[...].T[0:HEAD_ROWS, :]
        hi, mid, lo = _split3(xt)
        d_loc = (jnp.dot(hi, upper, preferred_element_type=F32) + jnp.dot(mid, upper, preferred_element_type=F32)
                 + jnp.dot(lo, upper, preferred_element_type=F32))
        d_ref[0, :, i * PAGE_SIZE:(i + 1) * PAGE_SIZE] = d_loc + carry_ref[...]
        carry_ref[...] = carry_ref[...] + jnp.sum(xt, axis=1, keepdims=True)


def _fox_bias(page_table, lf_new, pool_lf, pps):
    b, n_pages = page_table.shape
    npc = n_pages // pps
    page_specs = [pl.BlockSpec((1, PAGE_SIZE, H_B),
                               lambda bi, pc, pt, i=i: (pt[bi, (npc - 1 - pc) * pps + i], 0, 0))
                  for i in range(pps)]
    grid_spec = pltpu.PrefetchScalarGridSpec(
        num_scalar_prefetch=1,
        grid=(b, npc),
        in_specs=[pl.BlockSpec((1, 1, LANES), lambda bi, pc, pt: (bi, 0, 0))] + page_specs,
        out_specs=pl.BlockSpec((1, HEAD_ROWS, pps * PAGE_SIZE), lambda bi, pc, pt: (bi, 0, npc - 1 - pc)),
        scratch_shapes=[pltpu.VMEM((LANES, LANES), F32), pltpu.VMEM((HEAD_ROWS, 1), F32)],
    )
    return pl.pallas_call(
        functools.partial(_fox_bias_body, pps),
        grid_spec=grid_spec,
        out_shape=jax.ShapeDtypeStruct((b, HEAD_ROWS, n_pages * PAGE_SIZE), F32),
        compiler_params=_params(2, 32),
        name="fox_bias_sample",
    )(page_table, lf_new, *([pool_lf] * pps))


def _paged_body(pps, scale, pt_ref, q_ref, bias_ref, kn_ref, vn_ref, nb_ref, z_ref, *rest):
    kps, vps = rest[:pps], rest[pps:2 * pps]
    u_ref, m_ref, l_ref, acc_ref = rest[2 * pps:2 * pps + 4]
    pc = pl.program_id(1)
    q = q_ref[0]

    @pl.when(pc == 0)
    def _():
        kn = kn_ref[0].astype(BF16).astype(F32)
        vn = vn_ref[0].astype(BF16).astype(F32)
        nb = nb_ref[0]
        on = nb > 0.5 * NEG
        s0 = jnp.sum(q.astype(F32) * kn, axis=1, keepdims=True) * scale + nb
        m_ref[...] = jnp.where(on, s0, NEG)
        l_ref[...] = jnp.where(on, 1.0, 0.0)
        acc_ref[...] = jnp.where(on, jnp.broadcast_to(vn, acc_ref.shape), 0.0)

    for i in range(pps):
        s = _dot_nt(q, kps[i][0].astype(BF16)) * scale + bias_ref[0, :, i * PAGE_SIZE:(i + 1) * PAGE_SIZE]
        m_old = m_ref[...]
        m_new = jnp.maximum(m_old, jnp.max(s, axis=1, keepdims=True))
        alpha = jnp.exp(m_old - m_new)
        p = jnp.where(s > 0.5 * NEG, jnp.exp(s - m_new), 0.0)
        l_ref[...] = alpha * l_ref[...] + jnp.sum(p, axis=1, keepdims=True)
        acc_ref[...] = alpha * acc_ref[...] + jnp.dot(p.astype(BF16), vps[i][0].astype(BF16),
                                                      preferred_element_type=F32)
        m_ref[...] = m_new

    @pl.when(pc == pl.num_programs(1) - 1)
    def _():
        u_ref[0] = (acc_ref[...] / l_ref[...]) * _silu(z_ref[0])


def _paged_attn(page_idx, q_bd, bias, k_new, v_new, new_bias, z_bd, pool_k, pool_v, pps, scale, name):
    b, n_pages = page_idx.shape
    w = q_bd.shape[-1]
    rb = bias.shape[1]
    npc = n_pages // pps
    kspecs = [pl.BlockSpec((1, PAGE_SIZE, w), lambda bi, pc, pt, i=i: (pt[bi, pc * pps + i], 0, 0))
              for i in range(pps)]
    per_b = lambda bi, pc, pt: (bi, 0, 0)
    grid_spec = pltpu.PrefetchScalarGridSpec(
        num_scalar_prefetch=1,
        grid=(b, npc),
        in_specs=[pl.BlockSpec((1, HEAD_ROWS, w), per_b),
                  pl.BlockSpec((1, rb, pps * PAGE_SIZE), lambda bi, pc, pt: (bi, 0, pc)),
                  pl.BlockSpec((1, 1, w), per_b),
                  pl.BlockSpec((1, 1, w), per_b),
                  pl.BlockSpec((1, HEAD_ROWS, 1), per_b),
                  pl.BlockSpec((1, HEAD_ROWS, w), per_b)] + kspecs + kspecs,
        out_specs=pl.BlockSpec((1, HEAD_ROWS, w), per_b),
        scratch_shapes=[pltpu.VMEM((HEAD_ROWS, 1), F32), pltpu.VMEM((HEAD_ROWS, 1), F32),
                        pltpu.VMEM((HEAD_ROWS, w), F32)],
    )
    return pl.pallas_call(
        functools.partial(_paged_body, pps, scale),
        grid_spec=grid_spec,
        out_shape=jax.ShapeDtypeStruct((b, HEAD_ROWS, w), F32),
        compiler_params=_params(2, 48),
        name=name,
    )(page_idx, q_bd, bias, k_new, v_new, new_bias, z_bd, *([pool_k] * pps), *([pool_v] * pps))


def _rotary_tables(pos):
    def base(width):
        rot = width // ROT_FRAC
        half = rot // 2
        inv = jnp.power(jnp.float32(ROPE_THETA), -jnp.arange(half, dtype=F32) * (2.0 / rot))
        ang = pos.astype(F32)[:, None] * inv[None, :]
        cos, sin = jnp.cos(ang), jnp.sin(ang)
        n = pos.shape[0]
        ones, zeros = jnp.ones((n, width - rot), F32), jnp.zeros((n, width - rot), F32)
        zh = jnp.zeros((n, half), F32)
        return (jnp.concatenate([cos, cos, ones], 1), jnp.concatenate([-sin, zh, zeros], 1),
                jnp.concatenate([zh, sin, zeros], 1))

    c1, a1, b1 = base(HEAD_DIM)
    c2, a2, b2 = base(D_IDX)
    n = pos.shape[0]
    pad1, pad0 = jnp.ones((n, LANES - D_IDX), F32), jnp.zeros((n, LANES - D_IDX), F32)
    return jnp.stack([c1, a1, b1,
                      jnp.concatenate([c2, c2], 1), jnp.concatenate([a2, a2], 1), jnp.concatenate([b2, b2], 1),
                      jnp.concatenate([c2, pad1], 1), jnp.concatenate([a2, pad0], 1), jnp.concatenate([b2, pad0], 1)])


def _prep_w_in(w_in):
    pts = np.concatenate([[0], np.cumsum((W_A, W_KV, W_KV, H_IDX * D_IDX, D_IDX, H_IDX, W_A, W_B, W_KV, W_KV, H_B,
                                          W_B, W_C, W_C, 3 * D_MODEL))])
    seg = [w_in[:, int(pts[i]):int(pts[i + 1])].astype(BF16) for i in range(15)]
    qa, ka, va, iq, ik, iw, za, qb, kb, vb, fb, zb, qc, zc, gates = seg
    pad = jnp.zeros((w_in.shape[0], SMALL_W - D_IDX - H_IDX - H_B), BF16)
    return jnp.concatenate([qa, qb, za, zb, ka, va, kb, vb, iq, qc, zc, gates, ik, iw, fb, pad], axis=1)


def _block_diag(x, n_heads, n_groups, hd):
    b = x.shape[0]
    xh = x.reshape(b, n_heads, 1, hd)
    gid = jnp.arange(n_heads) // (n_heads // n_groups)
    onehot = (gid[:, None] == jnp.arange(n_groups)[None, :]).astype(x.dtype)
    bd = (xh * onehot[None, :, :, None]).reshape(b, n_heads, n_groups * hd)
    return jnp.pad(bd, ((0, 0), (0, HEAD_ROWS - n_heads), (0, 0)))


def _diag_blocks(u_full, n_heads, n_groups, hd):
    b = u_full.shape[0]
    gid = jnp.arange(n_heads) // (n_heads // n_groups)
    u = u_full.reshape(b, HEAD_ROWS, n_groups, hd)[:, jnp.arange(n_heads), gid]
    return u.reshape(b, n_heads * hd)


def _trunk(x2d, proj, ua, ub, uc, w_a, w_b, w_c, w_o, g_final):
    h = _merge(ua, ub, uc, w_a, w_b, w_c, proj, 1024, 512)
    y = _outproj(x2d, h, w_o, 1024, 512)
    return _rmsnorm(y, g_final, F32, 256)


def kernel(x_prompt, x_sample, cache_a_k, cache_a_v, cache_a_idx, cache_b_k, cache_b_v, cache_b_logf, cache_mem_k,
           cache_mem_v, page_table, mem_prompt, g_norm, w_in, b_forget, w_br_a, w_br_b, w_br_c, w_out, g_mem,
           w_mem_kv, g_final):
    off, n_cols = _layout()
    bsz, seq, d = x_prompt.shape
    dec = x_sample.shape[0]
    tq = min(256, seq)

    w_in_bf = _prep_w_in(w_in[0])
    w_a, w_b, w_c, w_o = (w_br_a[0].astype(BF16), w_br_b[0].astype(BF16), w_br_c[0].astype(BF16),
                          w_out[0].astype(BF16))
    bf_row = jnp.zeros((1, LANES), F32).at[0, LF_LANE:LF_LANE + H_B].set(b_forget[0])

    xp = x_prompt.reshape(bsz * seq, d)
    proj = _matmul(_rmsnorm(xp, g_norm[0], BF16, 256), w_in_bf, 1024, 512, "in_proj")
    tab_p = _rotary_tables(jnp.arange(seq, dtype=jnp.int32))
    ka_rot, small = _post(proj, tab_p, bf_row, seq, 512)
    col = lambda name, width: proj[:, off[name]:off[name] + width]
    va, kb, vb = col("va", W_KV), col("kb", W_KV), col("vb", W_KV)
    ik = small[:, :D_IDX]
    logf = small[:, LF_LANE:LF_LANE + H_B]

    mem_n = _rmsnorm(mem_prompt.reshape(bsz * N_MEM, d), g_mem[0], BF16, 256)
    mkv = _matmul(mem_n, w_mem_kv[0].astype(BF16), 1024, 1024, "mem_kv")
    mk, mv = mkv[:, :W_C], mkv[:, W_C:]

    ik_bf = ik.astype(BF16)
    ik_pad = jnp.zeros_like(ik_bf)
    ik2 = jnp.concatenate([ik_bf, ik_pad, ik_pad, ik_bf], axis=1)
    ua = _dsa_prompt(proj, small, ka_rot.astype(BF16), va.astype(BF16), ik2, tab_p, bsz, seq, tq)
    ccol, crow = _cumsum(small, bsz, seq)
    ub = _fox_prompt(proj, kb.astype(BF16), vb.astype(BF16), ccol, crow, bsz, seq, tq)
    uc = _cross_prompt(proj, mk.astype(BF16), mv.astype(BF16), bsz, seq, tq)
    y_prompt = _trunk(xp, proj, ua, ub, uc, w_a, w_b, w_c, w_o, g_final).reshape(bsz, seq, d)

    xs = x_sample.reshape(dec, d)
    proj_s = _matmul(_rmsnorm(xs, g_norm[0], BF16, dec), w_in_bf, dec, 512, "in_proj_sample")
    tab_s = _rotary_tables(jnp.full((dec,), page_table.shape[1] * PAGE_SIZE, jnp.int32))
    ka_s, small_s = _post(proj_s, tab_s, bf_row, dec, dec)
    cols = lambda name, width: proj_s[:, off[name]:off[name] + width]
    va_s, kb_s, vb_s = cols("va", W_KV), cols("kb", W_KV), cols("vb", W_KV)
    ik_s = small_s[:, :D_IDX]
    iw_s = small_s[:, D_IDX:D_IDX + H_IDX]
    logf_s = small_s[:, LF_LANE:LF_LANE + H_B]
    qa_s, iq_s = _qrot(proj_s, tab_s)

    n_pages = page_table.shape[1]
    n_pool = cache_a_k.shape[1]
    pps = min(8, n_pages)
    scores, own = _idx_sample(page_table, iq_s.astype(BF16).reshape(dec, H_IDX, D_IDX), iw_s.reshape(dec, H_IDX, 1),
                              ik_s.astype(BF16).reshape(dec, 1, D_IDX), cache_a_idx[0], pps)
    n_keys = n_pages * PAGE_SIZE
    sel_bias = _topk_sample(jnp.concatenate([scores[:, 0], own[:, 0]], axis=1), n_keys + 1)

    scale = HEAD_DIM ** -0.5
    ua_full = _paged_attn(
        page_table, _block_diag(qa_s, H_A, KVH_A, HEAD_DIM).astype(BF16), sel_bias[:, None, :n_keys],
        ka_s.reshape(dec, 1, W_KV), va_s.reshape(dec, 1, W_KV),
        jnp.broadcast_to(sel_bias[:, None, n_keys:n_keys + 1], (dec, HEAD_ROWS, 1)),
        _block_diag(cols("za", W_A), H_A, KVH_A, HEAD_DIM),
        cache_a_k[0].reshape(n_pool, PAGE_SIZE, W_KV), cache_a_v[0].reshape(n_pool, PAGE_SIZE, W_KV),
        pps, scale, "dsa_sample")

    lf_new = jnp.pad(logf_s, ((0, 0), (0, LANES - H_B))).reshape(dec, 1, LANES)
    d_bias = _fox_bias(page_table, lf_new, cache_b_logf[0], pps)
    ub_full = _paged_attn(
        page_table, _block_diag(cols("qb", W_B), H_B, KVH_B, HEAD_DIM).astype(BF16), d_bias,
        kb_s.reshape(dec, 1, W_KV), vb_s.reshape(dec, 1, W_KV), jnp.zeros((dec, HEAD_ROWS, 1), F32),
        _block_diag(cols("zb", W_B), H_B, KVH_B, HEAD_DIM),
        cache_b_k[0].reshape(n_pool, PAGE_SIZE, W_KV), cache_b_v[0].reshape(n_pool, PAGE_SIZE, W_KV),
        pps, scale, "fox_sample")

    mem_pages = N_MEM // PAGE_SIZE
    mem_idx = (jnp.arange(dec, dtype=jnp.int32)[:, None] * mem_pages + jnp.arange(mem_pages, dtype=jnp.int32)[None, :])
    uc_full = _paged_attn(
        mem_idx, _block_diag(cols("qc", W_C), H_C, H_C, HD_C).astype(BF16), jnp.zeros((dec, 1, N_MEM), F32),
        jnp.zeros((dec, 1, W_C), F32), jnp.zeros((dec, 1, W_C), F32), jnp.full((dec, HEAD_ROWS, 1), NEG, F32),
        _block_diag(cols("zc", W_C), H_C, H_C, HD_C),
        cache_mem_k[0].reshape(dec * mem_pages, PAGE_SIZE, W_C), cache_mem_v[0].reshape(dec * mem_pages, PAGE_SIZE, W_C),
        mem_pages, HD_C ** -0.5, "cross_sample")

    ua_s = _diag_blocks(ua_full, H_A, KVH_A, HEAD_DIM).astype(BF16)
    ub_s = _diag_blocks(ub_full, H_B, KVH_B, HEAD_DIM).astype(BF16)
    uc_s = _diag_blocks(uc_full, H_C, H_C, HD_C).astype(BF16)
    y_sample = _trunk(xs, proj_s, ua_s, ub_s, uc_s, w_a, w_b, w_c, w_o, g_final).reshape(dec, 1, d)

    st = lambda a, b, t, *tail: a.reshape((1, b, t) + tail)
    return (y_prompt, y_sample,
            st(ka_rot, bsz, seq, KVH_A, HEAD_DIM), st(va, bsz, seq, KVH_A, HEAD_DIM), st(ik, bsz, seq, D_IDX),
            st(kb, bsz, seq, KVH_B, HEAD_DIM), st(vb, bsz, seq, KVH_B, HEAD_DIM), st(logf, bsz, seq, H_B),
            st(mk, bsz, N_MEM, H_C, HD_C), st(mv, bsz, N_MEM, H_C, HD_C),
            st(ka_s, dec, 1, KVH_A, HEAD_DIM), st(va_s, dec, 1, KVH_A, HEAD_DIM), st(ik_s, dec, 1, D_IDX),
            st(kb_s, dec, 1, KVH_B, HEAD_DIM), st(vb_s, dec, 1, KVH_B, HEAD_DIM), st(logf_s, dec, 1, H_B))
```

```python
import functools

import numpy as np
import jax
import jax.numpy as jnp
from jax import lax
from jax.experimental import pallas as pl
from jax.experimental.pallas import tpu as pltpu

D_MODEL = 4096
BATCH = 4
SEQ = 2048
DEC_BATCH = 32
PAST_LEN = 8192
PAGE_SIZE = 128
HEAD_DIM = 128
H_A = 12
KVH_A = 4
H_IDX = 16
D_IDX = 64
TOPK_MAX = 256
H_B = 12
KVH_B = 4
H_C = 4
HD_C = 256
N_MEM = 256
ROPE_THETA = 500000.0
ROT_FRAC = 4
EPS = 1e-6

F32 = jnp.float32
BF16 = jnp.bfloat16
NEG = -1e30
LANES = 128
HEAD_ROWS = 16
MIB = 1024 * 1024

W_A = H_A * HEAD_DIM
W_B = H_B * HEAD_DIM
W_C = H_C * HD_C
W_KV = KVH_A * HEAD_DIM
COL_TILE = 512
LF_LANE = D_IDX + H_IDX


def _layout():
    src, acc = {}, 0
    for n, s in zip(("qa", "ka", "va", "iq", "ik", "iw", "za", "qb", "kb", "vb", "fb", "zb", "qc", "zc", "gates"),
                    (W_A, W_KV, W_KV, H_IDX * D_IDX, D_IDX, H_IDX, W_A, W_B, W_KV, W_KV, H_B, W_B, W_C, W_C,
                     3 * D_MODEL)):
        src[n] = acc
        acc += s
    names = ("qa", "qb", "za", "zb", "ka", "va", "kb", "vb", "iq", "qc", "zc", "gates", "s1", "s2")
    sizes = (W_A, W_B, W_A, W_B, W_KV, W_KV, W_KV, W_KV, H_IDX * D_IDX, W_C, W_C, 3 * D_MODEL, COL_TILE, COL_TILE)
    starts = dict(src, s1=src["ik"], s2=src["fb"])
    off, tiles, acc = {}, [], 0
    for n, s in zip(names, sizes):
        off[n] = acc
        acc += s
        tiles += [starts[n] + COL_TILE * i for i in range(s // COL_TILE)]
    return off, acc, tiles


def _params(n_axes, vmem_mib):
    return pltpu.CompilerParams(dimension_semantics=("arbitrary",) * n_axes,
                                vmem_limit_bytes=vmem_mib * MIB)


def _sigmoid(z):
    return 1.0 / (1.0 + jnp.exp(-z))


def _silu(z):
    return z * _sigmoid(z)


def _log_sigmoid(z):
    return -(jnp.maximum(-z, 0.0) + jnp.log1p(jnp.exp(-jnp.abs(z))))


def _rot(x, c, sa, sb, half):
    n = x.shape[-1]
    return x * c + pltpu.roll(x, n - half, 1) * sa + pltpu.roll(x, half, 1) * sb


def _dot_nt(a, b):
    return lax.dot_general(a, b, (((1,), (1,)), ((), ())), preferred_element_type=F32)


def _split3(x):
    hi = x.astype(BF16)
    r = x - hi.astype(F32)
    mid = r.astype(BF16)
    lo = (r - mid.astype(F32)).astype(BF16)
    return hi, mid, lo


def _kth_largest_key(key, k):
    kf = jnp.float32(k)

    def count_ge(t):
        return jnp.sum((key >= t).astype(F32), axis=1, keepdims=True)

    int_min = jnp.int32(-2 ** 31)
    thr = jnp.where(count_ge(jnp.int32(0)) >= kf, jnp.int32(0), int_min)
    thr = jnp.broadcast_to(thr, (key.shape[0], 1)).astype(jnp.int32)

    def body(i, thr):
        cand = thr + jnp.left_shift(jnp.int32(1), jnp.int32(30) - i)
        return jnp.where(count_ge(cand) >= kf, cand, thr)

    return lax.fori_loop(0, 31, body, thr)


def _sortable_key(scores):
    bits = lax.bitcast_convert_type(scores + 0.0, jnp.int32)
    return jnp.where(bits < 0, bits ^ jnp.int32(0x7FFFFFFF), bits)


def _rmsnorm_body(x_ref, g_ref, o_ref):
    x = x_ref[...]
    ms = jnp.mean(x * x, axis=-1, keepdims=True)
    o_ref[...] = ((x * lax.rsqrt(ms + EPS)) * g_ref[...]).astype(o_ref.dtype)


def _rmsnorm(x, g, out_dtype, bm):
    m, d = x.shape
    bm = min(bm, m)
    return pl.pallas_call(
        _rmsnorm_body,
        grid=(m // bm,),
        in_specs=[pl.BlockSpec((bm, d), lambda i: (i, 0)), pl.BlockSpec((1, d), lambda i: (0, 0))],
        out_specs=pl.BlockSpec((bm, d), lambda i: (i, 0)),
        out_shape=jax.ShapeDtypeStruct((m, d), out_dtype),
        compiler_params=_params(1, 40),
        name="rmsnorm",
    )(x, g.reshape(1, d))


def _mm_body(x_ref, w_ref, o_ref):
    o_ref[...] = jnp.dot(x_ref[...], w_ref[...], preferred_element_type=F32)


def _matmul(x, w, bm, bn, name):
    m, k = x.shape
    n = w.shape[1]
    bm, bn = min(bm, m), min(bn, n)
    return pl.pallas_call(
        _mm_body,
        grid=(m // bm, n // bn),
        in_specs=[pl.BlockSpec((bm, k), lambda i, j: (i, 0)), pl.BlockSpec((k, bn), lambda i, j: (0, j))],
        out_specs=pl.BlockSpec((bm, bn), lambda i, j: (i, j)),
        out_shape=jax.ShapeDtypeStruct((m, n), F32),
        compiler_params=_params(2, 48),
        name=name,
    )(x, w)


def _wprep_body(shifts, tab_ref, a_ref, b_ref, o_ref):
    shift = tab_ref[1, pl.program_id(1)]

    @pl.when(shift == 0)
    def _():
        o_ref[...] = a_ref[...].astype(o_ref.dtype)

    for sh in shifts:
        @pl.when(shift == sh)
        def _(sh=sh):
            x = jnp.concatenate([a_ref[...], b_ref[...]], axis=1)
            o_ref[...] = pltpu.roll(x, 2 * COL_TILE - sh, 1)[:, :COL_TILE].astype(o_ref.dtype)


def _wprep(w, tiles, bk):
    k, n_src = w.shape
    last = pl.cdiv(n_src, COL_TILE) - 1
    tab = jnp.asarray([[c // COL_TILE for c in tiles], [c % COL_TILE for c in tiles]], jnp.int32)
    shifts = tuple(sorted({c % COL_TILE for c in tiles} - {0}))
    bk = min(bk, k)
    grid_spec = pltpu.PrefetchScalarGridSpec(
        num_scalar_prefetch=1,
        grid=(k // bk, len(tiles)),
        in_specs=[pl.BlockSpec((bk, COL_TILE), lambda i, t, tab: (i, tab[0, t])),
                  pl.BlockSpec((bk, COL_TILE), lambda i, t, tab: (i, jnp.minimum(tab[0, t] + 1, last)))],
        out_specs=pl.BlockSpec((bk, COL_TILE), lambda i, t, tab: (i, t)),
    )
    return pl.pallas_call(
        functools.partial(_wprep_body, shifts),
        grid_spec=grid_spec,
        out_shape=jax.ShapeDtypeStruct((k, COL_TILE * len(tiles)), BF16),
        compiler_params=_params(2, 40),
        name="w_prep",
    )(tab, w, w)


def _cast_body(x_ref, o_ref):
    o_ref[...] = x_ref[...].astype(o_ref.dtype)


def _cast_bf16(w, bm):
    m, n = w.shape
    return pl.pallas_call(
        _cast_body,
        grid=(m // bm,),
        in_specs=[pl.BlockSpec((bm, n), lambda i: (i, 0))],
        out_specs=pl.BlockSpec((bm, n), lambda i: (i, 0)),
        out_shape=jax.ShapeDtypeStruct((m, n), BF16),
        compiler_params=_params(1, 40),
        name="w_cast",
    )(w)


def _post_body(ka_ref, s1_ref, s2_ref, tab_ref, bf_ref, ka_o, sm_o):
    c, sa, sb = tab_ref[0], tab_ref[1], tab_ref[2]
    for h in range(KVH_A):
        sl = slice(h * HEAD_DIM, (h + 1) * HEAD_DIM)
        ka_o[:, sl] = _rot(ka_ref[:, sl], c, sa, sb, HEAD_DIM // ROT_FRAC // 2)
    r = _rot(s1_ref[...], tab_ref[6], tab_ref[7], tab_ref[8], D_IDX // ROT_FRAC // 2)
    lf = _log_sigmoid(pltpu.roll(s2_ref[...], LF_LANE, 1) + bf_ref[...])
    lane = lax.broadcasted_iota(jnp.int32, r.shape, 1)
    sm_o[...] = jnp.where(lane < LF_LANE, r, jnp.where(lane < LF_LANE + H_B, lf, 0.0))


def _post(proj, tab, bf_row, seq_rows, bm):
    off = _layout()[0]
    m = proj.shape[0]
    bm = min(bm, m)
    tb = seq_rows // bm
    return pl.pallas_call(
        _post_body,
        grid=(m // bm,),
        in_specs=[pl.BlockSpec((bm, W_KV), lambda i: (i, off["ka"] // W_KV)),
                  pl.BlockSpec((bm, LANES), lambda i: (i, off["s1"] // LANES)),
                  pl.BlockSpec((bm, LANES), lambda i: (i, off["s2"] // LANES)),
                  pl.BlockSpec((9, bm, LANES), lambda i: (0, i % tb, 0)),
                  pl.BlockSpec((1, LANES), lambda i: (0, 0))],
        out_specs=[pl.BlockSpec((bm, W_KV), lambda i: (i, 0)), pl.BlockSpec((bm, LANES), lambda i: (i, 0))],
        out_shape=[jax.ShapeDtypeStruct((m, W_KV), F32), jax.ShapeDtypeStruct((m, LANES), F32)],
        compiler_params=_params(1, 40),
        name="kv_post",
    )(proj, proj, proj, tab, bf_row)


def _qrot_body(qa_ref, iq_ref, tab_ref, qa_o, iq_o):
    for h in range(H_A):
        sl = slice(h * HEAD_DIM, (h + 1) * HEAD_DIM)
        qa_o[:, sl] = _rot(qa_ref[:, sl], tab_ref[0], tab_ref[1], tab_ref[2], HEAD_DIM // ROT_FRAC // 2)
    for hp in range(H_IDX * D_IDX // LANES):
        sl = slice(hp * LANES, (hp + 1) * LANES)
        iq_o[:, sl] = _rot(iq_ref[:, sl], tab_ref[3], tab_ref[4], tab_ref[5], D_IDX // ROT_FRAC // 2)


def _qrot(proj, tab):
    off = _layout()[0]
    m = proj.shape[0]
    wi = H_IDX * D_IDX
    return pl.pallas_call(
        _qrot_body,
        grid=(1,),
        in_specs=[pl.BlockSpec((m, W_A), lambda i: (0, off["qa"] // W_A)),
                  pl.BlockSpec((m, wi), lambda i: (0, off["iq"] // wi)),
                  pl.BlockSpec((9, m, LANES), lambda i: (0, 0, 0))],
        out_specs=[pl.BlockSpec((m, W_A), lambda i: (0, 0)), pl.BlockSpec((m, wi), lambda i: (0, 0))],
        out_shape=[jax.ShapeDtypeStruct((m, W_A), F32), jax.ShapeDtypeStruct((m, wi), F32)],
        compiler_params=_params(1, 32),
        name="q_rotary",
    )(proj, proj, tab)


def _cumsum_body(sm_ref, ccol_ref, crow_ref):
    r = lax.broadcasted_iota(jnp.int32, (LANES, LANES), 0)
    c = lax.broadcasted_iota(jnp.int32, (LANES, LANES), 1)
    tri = (c <= r).astype(BF16)
    keep = (c >= LF_LANE) & (c < LF_LANE + H_B)
    carry = jnp.zeros((1, LANES), F32)
    for blk in range(ccol_ref.shape[0] // LANES):
        rows = slice(blk * LANES, (blk + 1) * LANES)
        x = jnp.where(keep, sm_ref[rows, :], 0.0)
        hi, mid, lo = _split3(x)
        cb = (jnp.dot(tri, hi, preferred_element_type=F32) + jnp.dot(tri, mid, preferred_element_type=F32)
              + jnp.dot(tri, lo, preferred_element_type=F32)) + carry
        carry = cb[LANES - 1:LANES, :]
        ccol_ref[rows, :] = cb
        crow_ref[0, :, rows] = cb.T


def _cumsum(small, batch, seq):
    return pl.pallas_call(
        _cumsum_body,
        grid=(batch,),
        in_specs=[pl.BlockSpec((seq, LANES), lambda b: (b, 0))],
        out_specs=[pl.BlockSpec((seq, LANES), lambda b: (b, 0)), pl.BlockSpec((1, LANES, seq), lambda b: (b, 0, 0))],
        out_shape=[jax.ShapeDtypeStruct((batch * seq, LANES), F32), jax.ShapeDtypeStruct((batch, LANES, seq), F32)],
        compiler_params=_params(1, 32),
        name="fox_cumsum",
    )(small)


def _softmax_av(s, v):
    m = jnp.max(s, axis=1, keepdims=True)
    p = jnp.exp(s - m)
    l = jnp.sum(p, axis=1, keepdims=True)
    return jnp.dot(p.astype(BF16), v, preferred_element_type=F32) / l


def _dsa_prompt_body(topk, q_ref, iq_ref, sm_ref, z_ref, k_ref, v_ref, ik_ref, tab_ref, u_ref):
    tq = q_ref.shape[0]
    s_len = k_ref.shape[0]
    row0 = pl.program_id(1) * tq
    rows = row0 + lax.broadcasted_iota(jnp.int32, (tq, s_len), 0)
    cols = lax.broadcasted_iota(jnp.int32, (tq, s_len), 1)
    causal = cols <= rows

    acc = jnp.zeros((tq, s_len), F32)
    for hp in range(H_IDX * D_IDX // LANES):
        x = iq_ref[:, hp * LANES:(hp + 1) * LANES]
        xb = _rot(x, tab_ref[3], tab_ref[4], tab_ref[5], D_IDX // ROT_FRAC // 2).astype(BF16)
        for j in range(LANES // D_IDX):
            h = hp * (LANES // D_IDX) + j
            s = _dot_nt(xb, ik_ref[:, j * LANES:(j + 1) * LANES]) * (D_IDX ** -0.5)
            w = sm_ref[:, D_IDX + h:D_IDX + h + 1] * (H_IDX ** -0.5)
            acc = acc + jnp.maximum(s, 0.0) * w
    key = _sortable_key(jnp.where(causal, acc, -jnp.inf))
    sel = (key >= _kth_largest_key(key, topk)) & causal
    bias = jnp.where(sel, 0.0, NEG)

    grp = H_A // KVH_A
    for h in range(H_A):
        sl = slice(h * HEAD_DIM, (h + 1) * HEAD_DIM)
        g = h // grp
        gs = slice(g * HEAD_DIM, (g + 1) * HEAD_DIM)
        qh = _rot(q_ref[:, sl], tab_ref[0], tab_ref[1], tab_ref[2], HEAD_DIM // ROT_FRAC // 2).astype(BF16)
        s = _dot_nt(qh, k_ref[:, gs]) * (HEAD_DIM ** -0.5) + bias
        o = _softmax_av(s, v_ref[:, gs])
        u_ref[:, sl] = (o * _silu(z_ref[:, sl])).astype(u_ref.dtype)


def _dsa_prompt(proj, small, ka_bf, va_bf, ik_bf, tab, batch, seq, tq):
    off = _layout()[0]
    nq = seq // tq
    wi = H_IDX * D_IDX
    topk = min(TOPK_MAX, seq // 4)
    row = lambda b, q: b * nq + q
    return pl.pallas_call(
        functools.partial(_dsa_prompt_body, topk),
        grid=(batch, nq),
        in_specs=[pl.BlockSpec((tq, W_A), lambda b, q: (row(b, q), off["qa"] // W_A)),
                  pl.BlockSpec((tq, wi), lambda b, q: (row(b, q), off["iq"] // wi)),
                  pl.BlockSpec((tq, LANES), lambda b, q: (row(b, q), 0)),
                  pl.BlockSpec((tq, W_A), lambda b, q: (row(b, q), off["za"] // W_A)),
                  pl.BlockSpec((seq, W_KV), lambda b, q: (b, 0)),
                  pl.BlockSpec((seq, W_KV), lambda b, q: (b, 0)),
                  pl.BlockSpec((seq, 2 * LANES), lambda b, q: (b, 0)),
                  pl.BlockSpec((9, tq, LANES), lambda b, q: (0, q, 0))],
        out_specs=pl.BlockSpec((tq, W_A), lambda b, q: (row(b, q), 0)),
        out_shape=jax.ShapeDtypeStruct((batch * seq, W_A), BF16),
        compiler_params=_params(2, 56),
        name="dsa_prompt",
    )(proj, proj, small, proj, ka_bf, va_bf, ik_bf, tab)


def _fox_prompt_body(q_ref, z_ref, k_ref, v_ref, ccol_ref, crow_ref, u_ref):
    tq = q_ref.shape[0]
    s_len = k_ref.shape[0]
    row0 = pl.program_id(1) * tq
    rows = row0 + lax.broadcasted_iota(jnp.int32, (tq, s_len), 0)
    cols = lax.broadcasted_iota(jnp.int32, (tq, s_len), 1)
    bias = jnp.where(cols <= rows, 0.0, NEG)
    grp = H_B // KVH_B
    for h in range(H_B):
        sl = slice(h * HEAD_DIM, (h + 1) * HEAD_DIM)
        g = h // grp
        gs = slice(g * HEAD_DIM, (g + 1) * HEAD_DIM)
        s = _dot_nt(q_ref[:, sl].astype(BF16), k_ref[:, gs]) * (HEAD_DIM ** -0.5)
        s = s + ccol_ref[:, LF_LANE + h:LF_LANE + h + 1] - crow_ref[0, LF_LANE + h:LF_LANE + h + 1, :]
        o = _softmax_av(s + bias, v_ref[:, gs])
        u_ref[:, sl] = (o * _silu(z_ref[:, sl])).astype(u_ref.dtype)


def _fox_prompt(proj, kb_bf, vb_bf, ccol, crow, batch, seq, tq):
    off = _layout()[0]
    nq = seq // tq
    row = lambda b, q: b * nq + q
    return pl.pallas_call(
        _fox_prompt_body,
        grid=(batch, nq),
        in_specs=[pl.BlockSpec((tq, W_B), lambda b, q: (row(b, q), off["qb"] // W_B)),
                  pl.BlockSpec((tq, W_B), lambda b, q: (row(b, q), off["zb"] // W_B)),
                  pl.BlockSpec((seq, W_KV), lambda b, q: (b, 0)),
                  pl.BlockSpec((seq, W_KV), lambda b, q: (b, 0)),
                  pl.BlockSpec((tq, LANES), lambda b, q: (row(b, q), 0)),
                  pl.BlockSpec((1, LANES, seq), lambda b, q: (b, 0, 0))],
        out_specs=pl.BlockSpec((tq, W_B), lambda b, q: (row(b, q), 0)),
        out_shape=jax.ShapeDtypeStruct((batch * seq, W_B), BF16),
        compiler_params=_params(2, 56),
        name="fox_prompt",
    )(proj, proj, kb_bf, vb_bf, ccol, crow)


def _cross_prompt_body(q_ref, z_ref, k_ref, v_ref, u_ref):
    for h in range(H_C):
        sl = slice(h * HD_C, (h + 1) * HD_C)
        s = _dot_nt(q_ref[:, sl].astype(BF16), k_ref[:, sl]) * (HD_C ** -0.5)
        o = _softmax_av(s, v_ref[:, sl])
        u_ref[:, sl] = (o * _silu(z_ref[:, sl])).astype(u_ref.dtype)


def _cross_prompt(proj, mk_bf, mv_bf, batch, seq, tq):
    off = _layout()[0]
    nq = seq // tq
    row = lambda b, q: b * nq + q
    return pl.pallas_call(
        _cross_prompt_body,
        grid=(batch, nq),
        in_specs=[pl.BlockSpec((tq, W_C), lambda b, q: (row(b, q), off["qc"] // W_C)),
                  pl.BlockSpec((tq, W_C), lambda b, q: (row(b, q), off["zc"] // W_C)),
                  pl.BlockSpec((N_MEM, W_C), lambda b, q: (b, 0)),
                  pl.BlockSpec((N_MEM, W_C), lambda b, q: (b, 0))],
        out_specs=pl.BlockSpec((tq, W_C), lambda b, q: (row(b, q), 0)),
        out_shape=jax.ShapeDtypeStruct((batch * seq, W_C), BF16),
        compiler_params=_params(2, 40),
        name="cross_prompt",
    )(proj, proj, mk_bf, mv_bf)


def _merge_body(ua_ref, ub_ref, uc_ref, wa_ref, wb_ref, wc_ref, ga_ref, gb_ref, gc_ref, h_ref):
    h = (_sigmoid(ga_ref[...]) * jnp.dot(ua_ref[...], wa_ref[...], preferred_element_type=F32)
         + _sigmoid(gb_ref[...]) * jnp.dot(ub_ref[...], wb_ref[...], preferred_element_type=F32)
         + _sigmoid(gc_ref[...]) * jnp.dot(uc_ref[...], wc_ref[...], preferred_element_type=F32))
    h_ref[...] = h.astype(h_ref.dtype)


def _merge(ua, ub, uc, wa, wb, wc, proj, bm, bn):
    off = _layout()[0]
    m = ua.shape[0]
    d = wa.shape[1]
    bm, bn = min(bm, m), min(bn, d)
    g0 = off["gates"] // bn
    nd = d // bn
    return pl.pallas_call(
        _merge_body,
        grid=(m // bm, nd),
        in_specs=[pl.BlockSpec((bm, W_A), lambda i, j: (i, 0)),
                  pl.BlockSpec((bm, W_B), lambda i, j: (i, 0)),
                  pl.BlockSpec((bm, W_C), lambda i, j: (i, 0)),
                  pl.BlockSpec((W_A, bn), lambda i, j: (0, j)),
                  pl.BlockSpec((W_B, bn), lambda i, j: (0, j)),
                  pl.BlockSpec((W_C, bn), lambda i, j: (0, j)),
                  pl.BlockSpec((bm, bn), lambda i, j: (i, g0 + j)),
                  pl.BlockSpec((bm, bn), lambda i, j: (i, g0 + nd + j)),
                  pl.BlockSpec((bm, bn), lambda i, j: (i, g0 + 2 * nd + j))],
        out_specs=pl.BlockSpec((bm, bn), lambda i, j: (i, j)),
        out_shape=jax.ShapeDtypeStruct((m, d), BF16),
        compiler_params=_params(2, 48),
        name="merge",
    )(ua, ub, uc, wa, wb, wc, proj, proj, proj)


def _outproj_body(x_ref, h_ref, w_ref, y_ref):
    y_ref[...] = x_ref[...] + jnp.dot(h_ref[...], w_ref[...], preferred_element_type=F32)


def _outproj(x, h, w, bm, bn):
    m, d = x.shape
    bm, bn = min(bm, m), min(bn, d)
    return pl.pallas_call(
        _outproj_body,
        grid=(m // bm, d // bn),
        in_specs=[pl.BlockSpec((bm, bn), lambda i, j: (i, j)),
                  pl.BlockSpec((bm, d), lambda i, j: (i, 0)),
                  pl.BlockSpec((d, bn), lambda i, j: (0, j))],
        out_specs=pl.BlockSpec((bm, bn), lambda i, j: (i, j)),
        out_shape=jax.ShapeDtypeStruct((m, d), F32),
        compiler_params=_params(2, 48),
        name="out_proj",
    )(x, h, w)


def _idx_sample_body(pps, pt_ref, iq_ref, w_ref, ikn_ref, *rest):
    pages = rest[:pps]
    sc_ref, own_ref = rest[pps], rest[pps + 1]
    iq = iq_ref[0]
    w = w_ref[0] * (H_IDX ** -0.5)
    for i in range(pps):
        s = _dot_nt(iq, pages[i][...].astype(BF16)) * (D_IDX ** -0.5)
        sc_ref[0, :, i * PAGE_SIZE:(i + 1) * PAGE_SIZE] = jnp.sum(jnp.maximum(s, 0.0) * w, axis=0, keepdims=True)

    @pl.when(pl.program_id(1) == 0)
    def _():
        kn = ikn_ref[0].astype(F32)
        s = jnp.sum(iq.astype(F32) * kn, axis=1, keepdims=True) * (D_IDX ** -0.5)
        own = jnp.sum(jnp.maximum(s, 0.0) * w, axis=0, keepdims=True)
        lane = lax.broadcasted_iota(jnp.int32, (1, LANES), 1)
        own_ref[0] = jnp.where(lane == 0, own, -jnp.inf)


def _idx_sample(page_table, iq_bf, iw, ik_new_bf, pool_idx, pps):
    b, n_pages = page_table.shape
    npc = n_pages // pps
    page_specs = [pl.BlockSpec((None, None, PAGE_SIZE, D_IDX),
                               lambda bi, pc, pt, i=i: (0, pt[bi, pc * pps + i], 0, 0)) for i in range(pps)]
    grid_spec = pltpu.PrefetchScalarGridSpec(
        num_scalar_prefetch=1,
        grid=(b, npc),
        in_specs=[pl.BlockSpec((1, H_IDX, D_IDX), lambda bi, pc, pt: (bi, 0, 0)),
                  pl.BlockSpec((1, H_IDX, 1), lambda bi, pc, pt: (bi, 0, 0)),
                  pl.BlockSpec((1, 1, D_IDX), lambda bi, pc, pt: (bi, 0, 0))] + page_specs,
        out_specs=[pl.BlockSpec((1, 1, pps * PAGE_SIZE), lambda bi, pc, pt: (bi, 0, pc)),
                   pl.BlockSpec((1, 1, LANES), lambda bi, pc, pt: (bi, 0, 0))],
    )
    return pl.pallas_call(
        functools.partial(_idx_sample_body, pps),
        grid_spec=grid_spec,
        out_shape=[jax.ShapeDtypeStruct((b, 1, n_pages * PAGE_SIZE), F32),
                   jax.ShapeDtypeStruct((b, 1, LANES), F32)],
        compiler_params=_params(2, 32),
        name="idx_sample",
    )(page_table, iq_bf, iw, ik_new_bf, *([pool_idx] * pps))


def _topk_sample_body(topk, n_valid, sc_ref, bias_ref):
    sc = sc_ref[...]
    lane = lax.broadcasted_iota(jnp.int32, sc.shape, 1)
    valid = lane < n_valid
    key = _sortable_key(jnp.where(valid, sc, -jnp.inf))
    sel = (key >= _kth_largest_key(key, topk)) & valid
    bias_ref[...] = jnp.where(sel, 0.0, NEG)


def _topk_sample(scores, n_valid):
    topk = min(TOPK_MAX, n_valid // 4)
    return pl.pallas_call(
        functools.partial(_topk_sample_body, topk, n_valid),
        grid=(1,),
        in_specs=[pl.BlockSpec(scores.shape, lambda i: (0, 0))],
        out_specs=pl.BlockSpec(scores.shape, lambda i: (0, 0)),
        out_shape=jax.ShapeDtypeStruct(scores.shape, F32),
        compiler_params=_params(1, 32),
        name="topk_sample",
    )(scores)


def _fox_bias_body(pps, pt_ref, lfn_ref, *rest):
    pages = rest[:pps]
    d_ref, x_ref, carry_ref = rest[pps], rest[pps + 1], rest[pps + 2]
    r = lax.broadcasted_iota(jnp.int32, (LANES, LANES), 0)
    c = lax.broadcasted_iota(jnp.int32, (LANES, LANES), 1)
    upper = (r > c).astype(BF16)

    @pl.when(pl.program_id(1) == 0)
    def _():
        x_ref[...] = jnp.zeros_like(x_ref)
        x_ref[0:1, :] = lfn_ref[0]
        carry_ref[...] = x_ref[...].T[0:HEAD_ROWS, 0:1]
        x_ref[0:1, :] = jnp.zeros((1, LANES), F32)

    for i in reversed(range(pps)):
        x_ref[:, 0:H_B] = pages[i][...]
        xt = x_ref[...].T[0:HEAD_ROWS, :]
        hi, mid, lo = _split3(xt)
        d_loc = (jnp.dot(hi, upper, preferred_element_type=F32) + jnp.dot(mid, upper, preferred_element_type=F32)
                 + jnp.dot(lo, upper, preferred_element_type=F32))
        d_ref[0, :, i * PAGE_SIZE:(i + 1) * PAGE_SIZE] = d_loc + carry_ref[...]
        carry_ref[...] = carry_ref[...] + jnp.sum(xt, axis=1, keepdims=True)


def _fox_bias(page_table, lf_new, pool_lf, pps):
    b, n_pages = page_table.shape
    npc = n_pages // pps
    page_specs = [pl.BlockSpec((None, None, PAGE_SIZE, H_B),
                               lambda bi, pc, pt, i=i: (0, pt[bi, (npc - 1 - pc) * pps + i], 0, 0))
                  for i in range(pps)]
    grid_spec = pltpu.PrefetchScalarGridSpec(
        num_scalar_prefetch=1,
        grid=(b, npc),
        in_specs=[pl.BlockSpec((1, 1, LANES), lambda bi, pc, pt: (bi, 0, 0))] + page_specs,
        out_specs=pl.BlockSpec((1, HEAD_ROWS, pps * PAGE_SIZE), lambda bi, pc, pt: (bi, 0, npc - 1 - pc)),
        scratch_shapes=[pltpu.VMEM((LANES, LANES), F32), pltpu.VMEM((HEAD_ROWS, 1), F32)],
    )
    return pl.pallas_call(
        functools.partial(_fox_bias_body, pps),
        grid_spec=grid_spec,
        out_shape=jax.ShapeDtypeStruct((b, HEAD_ROWS, n_pages * PAGE_SIZE), F32),
        compiler_params=_params(2, 32),
        name="fox_bias_sample",
    )(page_table, lf_new, *([pool_lf] * pps))


def _paged_body(pps, scale, pt_ref, q_ref, bias_ref, kn_ref, vn_ref, nb_ref, z_ref, *rest):
    kps, vps = rest[:pps], rest[pps:2 * pps]
    u_ref, m_ref, l_ref, acc_ref = rest[2 * pps:2 * pps + 4]
    pc = pl.program_id(1)
    n_grp, hd = kps[0].shape[1], kps[0].shape[2]
    q = q_ref[0]

    @pl.when(pc == 0)
    def _():
        kn = kn_ref[0].astype(BF16).astype(F32)
        vn = vn_ref[0].astype(BF16).astype(F32)
        nb = nb_ref[0]
        on = nb > 0.5 * NEG
        s0 = jnp.sum(q.astype(F32) * kn, axis=1, keepdims=True) * scale + nb
        m_ref[...] = jnp.where(on, s0, NEG)
        l_ref[...] = jnp.where(on, 1.0, 0.0)
        acc_ref[...] = jnp.where(on, jnp.broadcast_to(vn, acc_ref.shape), 0.0)

    for i in range(pps):
        qk = _dot_nt(q[:, 0:hd], kps[i][:, 0, :].astype(BF16))
        for g in range(1, n_grp):
            qk = qk + _dot_nt(q[:, g * hd:(g + 1) * hd], kps[i][:, g, :].astype(BF16))
        s = qk * scale + bias_ref[0, :, i * PAGE_SIZE:(i + 1) * PAGE_SIZE]
        m_old = m_ref[...]
        m_new = jnp.maximum(m_old, jnp.max(s, axis=1, keepdims=True))
        alpha = jnp.exp(m_old - m_new)
        pf = jnp.where(s > 0.5 * NEG, jnp.exp(s - m_new), 0.0)
        l_ref[...] = alpha * l_ref[...] + jnp.sum(pf, axis=1, keepdims=True)
        p = pf.astype(BF16)
        for g in range(n_grp):
            gs = slice(g * hd, (g + 1) * hd)
            acc_ref[:, gs] = alpha * acc_ref[:, gs] + jnp.dot(p, vps[i][:, g, :].astype(BF16),
                                                              preferred_element_type=F32)
        m_ref[...] = m_new

    @pl.when(pc == pl.num_programs(1) - 1)
    def _():
        u_ref[0] = (acc_ref[...] / l_ref[...]) * _silu(z_ref[0])


def _paged_attn(page_idx, q_bd, bias, k_new, v_new, new_bias, z_bd, pool_k, pool_v, pps, scale, name, paged=True):
    b, n_pages = page_idx.shape
    w = q_bd.shape[-1]
    rb = bias.shape[1]
    npc = n_pages // pps
    blk = (None, None, PAGE_SIZE) + pool_k.shape[3:]
    if paged:
        kspecs = [pl.BlockSpec(blk, lambda bi, pc, pt, i=i: (0, pt[bi, pc * pps + i], 0, 0, 0)) for i in range(pps)]
    else:
        kspecs = [pl.BlockSpec(blk, lambda bi, pc, pt, i=i: (0, bi, pc * pps + i, 0, 0)) for i in range(pps)]
    per_b = lambda bi, pc, pt: (bi, 0, 0)
    grid_spec = pltpu.PrefetchScalarGridSpec(
        num_scalar_prefetch=1,
        grid=(b, npc),
        in_specs=[pl.BlockSpec((1, HEAD_ROWS, w), per_b),
                  pl.BlockSpec((1, rb, pps * PAGE_SIZE), lambda bi, pc, pt: (bi, 0, pc)),
                  pl.BlockSpec((1, 1, w), per_b),
                  pl.BlockSpec((1, 1, w), per_b),
                  pl.BlockSpec((1, HEAD_ROWS, 1), per_b),
                  pl.BlockSpec((1, HEAD_ROWS, w), per_b)] + kspecs + kspecs,
        out_specs=pl.BlockSpec((1, HEAD_ROWS, w), per_b),
        scratch_shapes=[pltpu.VMEM((HEAD_ROWS, 1), F32), pltpu.VMEM((HEAD_ROWS, 1), F32),
                        pltpu.VMEM((HEAD_ROWS, w), F32)],
    )
    return pl.pallas_call(
        functools.partial(_paged_body, pps, scale),
        grid_spec=grid_spec,
        out_shape=jax.ShapeDtypeStruct((b, HEAD_ROWS, w), F32),
        compiler_params=_params(2, 48),
        name=name,
    )(page_idx, q_bd, bias, k_new, v_new, new_bias, z_bd, *([pool_k] * pps), *([pool_v] * pps))


def _rotary_tables(pos):
    def base(width):
        rot = width // ROT_FRAC
        half = rot // 2
        inv = jnp.power(jnp.float32(ROPE_THETA), -jnp.arange(half, dtype=F32) * (2.0 / rot))
        ang = pos.astype(F32)[:, None] * inv[None, :]
        cos, sin = jnp.cos(ang), jnp.sin(ang)
        n = pos.shape[0]
        ones, zeros = jnp.ones((n, width - rot), F32), jnp.zeros((n, width - rot), F32)
        zh = jnp.zeros((n, half), F32)
        return (jnp.concatenate([cos, cos, ones], 1), jnp.concatenate([-sin, zh, zeros], 1),
                jnp.concatenate([zh, sin, zeros], 1))

    c1, a1, b1 = base(HEAD_DIM)
    c2, a2, b2 = base(D_IDX)
    n = pos.shape[0]
    pad1, pad0 = jnp.ones((n, LANES - D_IDX), F32), jnp.zeros((n, LANES - D_IDX), F32)
    return jnp.stack([c1, a1, b1,
                      jnp.concatenate([c2, c2], 1), jnp.concatenate([a2, a2], 1), jnp.concatenate([b2, b2], 1),
                      jnp.concatenate([c2, pad1], 1), jnp.concatenate([a2, pad0], 1), jnp.concatenate([b2, pad0], 1)])


def _block_diag(x, n_heads, n_groups, hd):
    b = x.shape[0]
    xh = x.reshape(b, n_heads, 1, hd)
    gid = jnp.arange(n_heads) // (n_heads // n_groups)
    onehot = (gid[:, None] == jnp.arange(n_groups)[None, :]).astype(x.dtype)
    bd = (xh * onehot[None, :, :, None]).reshape(b, n_heads, n_groups * hd)
    return jnp.pad(bd, ((0, 0), (0, HEAD_ROWS - n_heads), (0, 0)))


def _diag_blocks(u_full, n_heads, n_groups, hd):
    b = u_full.shape[0]
    gid = jnp.arange(n_heads) // (n_heads // n_groups)
    u = u_full.reshape(b, HEAD_ROWS, n_groups, hd)[:, jnp.arange(n_heads), gid]
    return u.reshape(b, n_heads * hd)


def _trunk(x2d, proj, ua, ub, uc, w_a, w_b, w_c, w_o, g_final):
    h = _merge(ua, ub, uc, w_a, w_b, w_c, proj, 1024, 512)
    y = _outproj(x2d, h, w_o, 1024, 512)
    return _rmsnorm(y, g_final, F32, 256)


def kernel(x_prompt, x_sample, cache_a_k, cache_a_v, cache_a_idx, cache_b_k, cache_b_v, cache_b_logf, cache_mem_k,
           cache_mem_v, page_table, mem_prompt, g_norm, w_in, b_forget, w_br_a, w_br_b, w_br_c, w_out, g_mem,
           w_mem_kv, g_final):
    off, _, tiles = _layout()
    bsz, seq, d = x_prompt.shape
    dec = x_sample.shape[0]
    tq = min(256, seq)

    w_in_bf = _wprep(w_in[0], tiles, 2048)
    w_a, w_b, w_c, w_o, w_m = (_cast_bf16(w[0], 256) for w in (w_br_a, w_br_b, w_br_c, w_out, w_mem_kv))
    bf_row = jnp.zeros((1, LANES), F32).at[0, LF_LANE:LF_LANE + H_B].set(b_forget[0])

    xp = x_prompt.reshape(bsz * seq, d)
    proj = _matmul(_rmsnorm(xp, g_norm[0], BF16, 256), w_in_bf, 1024, 512, "in_proj")
    tab_p = _rotary_tables(jnp.arange(seq, dtype=jnp.int32))
    ka_rot, small = _post(proj, tab_p, bf_row, seq, 512)
    col = lambda name, width: proj[:, off[name]:off[name] + width]
    va, kb, vb = col("va", W_KV), col("kb", W_KV), col("vb", W_KV)
    ik = small[:, :D_IDX]
    logf = small[:, LF_LANE:LF_LANE + H_B]

    mem_n = _rmsnorm(mem_prompt.reshape(bsz * N_MEM, d), g_mem[0], BF16, 256)
    mkv = _matmul(mem_n, w_m, 1024, 1024, "mem_kv")
    mk, mv = mkv[:, :W_C], mkv[:, W_C:]

    ik_bf = ik.astype(BF16)
    ik_pad = jnp.zeros_like(ik_bf)
    ik2 = jnp.concatenate([ik_bf, ik_pad, ik_pad, ik_bf], axis=1)
    ua = _dsa_prompt(proj, small, ka_rot.astype(BF16), va.astype(BF16), ik2, tab_p, bsz, seq, tq)
    ccol, crow = _cumsum(small, bsz, seq)
    ub = _fox_prompt(proj, kb.astype(BF16), vb.astype(BF16), ccol, crow, bsz, seq, tq)
    uc = _cross_prompt(proj, mk.astype(BF16), mv.astype(BF16), bsz, seq, tq)
    y_prompt = _trunk(xp, proj, ua, ub, uc, w_a, w_b, w_c, w_o, g_final).reshape(bsz, seq, d)

    xs = x_sample.reshape(dec, d)
    proj_s = _matmul(_rmsnorm(xs, g_norm[0], BF16, dec), w_in_bf, dec, 512, "in_proj_sample")
    tab_s = _rotary_tables(jnp.full((dec,), page_table.shape[1] * PAGE_SIZE, jnp.int32))
    ka_s, small_s = _post(proj_s, tab_s, bf_row, dec, dec)
    cols = lambda name, width: proj_s[:, off[name]:off[name] + width]
    va_s, kb_s, vb_s = cols("va", W_KV), cols("kb", W_KV), cols("vb", W_KV)
    ik_s = small_s[:, :D_IDX]
    iw_s = small_s[:, D_IDX:D_IDX + H_IDX]
    logf_s = small_s[:, LF_LANE:LF_LANE + H_B]
    qa_s, iq_s = _qrot(proj_s, tab_s)

    n_pages = page_table.shape[1]
    pps = min(8, n_pages)
    pps_small = min(32, n_pages)
    scores, own = _idx_sample(page_table, iq_s.astype(BF16).reshape(dec, H_IDX, D_IDX), iw_s.reshape(dec, H_IDX, 1),
                              ik_s.astype(BF16).reshape(dec, 1, D_IDX), cache_a_idx, pps_small)
    n_keys = n_pages * PAGE_SIZE
    sel_bias = _topk_sample(jnp.concatenate([scores[:, 0], own[:, 0]], axis=1), n_keys + 1)

    scale = HEAD_DIM ** -0.5
    ua_full = _paged_attn(
        page_table, _block_diag(qa_s, H_A, KVH_A, HEAD_DIM).astype(BF16), sel_bias[:, None, :n_keys],
        ka_s.reshape(dec, 1, W_KV), va_s.reshape(dec, 1, W_KV),
        jnp.broadcast_to(sel_bias[:, None, n_keys:n_keys + 1], (dec, HEAD_ROWS, 1)),
        _block_diag(cols("za", W_A), H_A, KVH_A, HEAD_DIM),
        cache_a_k, cache_a_v, pps, scale, "dsa_sample")

    lf_new = jnp.pad(logf_s, ((0, 0), (0, LANES - H_B))).reshape(dec, 1, LANES)
    d_bias = _fox_bias(page_table, lf_new, cache_b_logf, min(16, n_pages))
    ub_full = _paged_attn(
        page_table, _block_diag(cols("qb", W_B), H_B, KVH_B, HEAD_DIM).astype(BF16), d_bias,
        kb_s.reshape(dec, 1, W_KV), vb_s.reshape(dec, 1, W_KV), jnp.zeros((dec, HEAD_ROWS, 1), F32),
        _block_diag(cols("zb", W_B), H_B, KVH_B, HEAD_DIM),
        cache_b_k, cache_b_v, pps, scale, "fox_sample")

    mem_pages = N_MEM // PAGE_SIZE
    uc_full = _paged_attn(
        jnp.zeros((dec, mem_pages), jnp.int32), _block_diag(cols("qc", W_C), H_C, H_C, HD_C).astype(BF16),
        jnp.zeros((dec, 1, N_MEM), F32),
        jnp.zeros((dec, 1, W_C), F32), jnp.zeros((dec, 1, W_C), F32), jnp.full((dec, HEAD_ROWS, 1), NEG, F32),
        _block_diag(cols("zc", W_C), H_C, H_C, HD_C),
        cache_mem_k, cache_mem_v, mem_pages, HD_C ** -0.5, "cross_sample", paged=False)

    ua_s = _diag_blocks(ua_full, H_A, KVH_A, HEAD_DIM).astype(BF16)
    ub_s = _diag_blocks(ub_full, H_B, KVH_B, HEAD_DIM).astype(BF16)
    uc_s = _diag_blocks(uc_full, H_C, H_C, HD_C).astype(BF16)
    y_sample = _trunk(xs, proj_s, ua_s, ub_s, uc_s, w_a, w_b, w_c, w_o, g_final).reshape(dec, 1, d)

    st = lambda a, b, t, *tail: a.reshape((1, b, t) + tail)
    return (y_prompt, y_sample,
            st(ka_rot, bsz, seq, KVH_A, HEAD_DIM), st(va, bsz, seq, KVH_A, HEAD_DIM), st(ik, bsz, seq, D_IDX),
            st(kb, bsz, seq, KVH_B, HEAD_DIM), st(vb, bsz, seq, KVH_B, HEAD_DIM), st(logf, bsz, seq, H_B),
            st(mk, bsz, N_MEM, H_C, HD_C), st(mv, bsz, N_MEM, H_C, HD_C),
            st(ka_s, dec, 1, KVH_A, HEAD_DIM), st(va_s, dec, 1, KVH_A, HEAD_DIM), st(ik_s, dec, 1, D_IDX),
            st(kb_s, dec, 1, KVH_B, HEAD_DIM), st(vb_s, dec, 1, KVH_B, HEAD_DIM), st(logf_s, dec, 1, H_B))
```

```python
import functools

import numpy as np
import jax
import jax.numpy as jnp
from jax import lax
from jax.experimental import pallas as pl
from jax.experimental.pallas import tpu as pltpu

D_MODEL = 4096
BATCH = 4
SEQ = 2048
DEC_BATCH = 32
PAST_LEN = 8192
PAGE_SIZE = 128
HEAD_DIM = 128
H_A = 12
KVH_A = 4
H_IDX = 16
D_IDX = 64
TOPK_MAX = 256
H_B = 12
KVH_B = 4
H_C = 4
HD_C = 256
N_MEM = 256
ROPE_THETA = 500000.0
ROT_FRAC = 4
EPS = 1e-6

F32 = jnp.float32
BF16 = jnp.bfloat16
NEG = -1e30
LANES = 128
HEAD_ROWS = 16
MIB = 1024 * 1024

W_A = H_A * HEAD_DIM
W_B = H_B * HEAD_DIM
W_C = H_C * HD_C
W_KV = KVH_A * HEAD_DIM
COL_TILE = 512
LF_LANE = D_IDX + H_IDX


def _layout():
    src, acc = {}, 0
    for n, s in zip(("qa", "ka", "va", "iq", "ik", "iw", "za", "qb", "kb", "vb", "fb", "zb", "qc", "zc", "gates"),
                    (W_A, W_KV, W_KV, H_IDX * D_IDX, D_IDX, H_IDX, W_A, W_B, W_KV, W_KV, H_B, W_B, W_C, W_C,
                     3 * D_MODEL)):
        src[n] = acc
        acc += s
    names = ("qa", "qb", "za", "zb", "ka", "va", "kb", "vb", "iq", "qc", "zc", "gates", "s1", "s2")
    sizes = (W_A, W_B, W_A, W_B, W_KV, W_KV, W_KV, W_KV, H_IDX * D_IDX, W_C, W_C, 3 * D_MODEL, COL_TILE, COL_TILE)
    starts = dict(src, s1=src["ik"], s2=src["fb"])
    off, tiles, acc = {}, [], 0
    for n, s in zip(names, sizes):
        off[n] = acc
        acc += s
        tiles += [starts[n] + COL_TILE * i for i in range(s // COL_TILE)]
    return off, acc, tiles


def _params(n_axes, vmem_mib):
    return pltpu.CompilerParams(dimension_semantics=("arbitrary",) * n_axes,
                                vmem_limit_bytes=vmem_mib * MIB)


def _sigmoid(z):
    return 1.0 / (1.0 + jnp.exp(-z))


def _silu(z):
    return z * _sigmoid(z)


def _log_sigmoid(z):
    return -(jnp.maximum(-z, 0.0) + jnp.log1p(jnp.exp(-jnp.abs(z))))


def _rot(x, c, sa, sb, half):
    n = x.shape[-1]
    return x * c + pltpu.roll(x, n - half, 1) * sa + pltpu.roll(x, half, 1) * sb


def _dot_nt(a, b):
    return lax.dot_general(a, b, (((1,), (1,)), ((), ())), preferred_element_type=F32)


def _split3(x):
    hi = x.astype(BF16)
    r = x - hi.astype(F32)
    mid = r.astype(BF16)
    lo = (r - mid.astype(F32)).astype(BF16)
    return hi, mid, lo


def _kth_largest_key(key, k):
    kf = jnp.float32(k)

    def count_ge(t):
        return jnp.sum((key >= t).astype(F32), axis=1, keepdims=True)

    int_min = jnp.int32(-2 ** 31)
    thr = jnp.where(count_ge(jnp.int32(0)) >= kf, jnp.int32(0), int_min)
    thr = jnp.broadcast_to(thr, (key.shape[0], 1)).astype(jnp.int32)

    def body(i, thr):
        cand = thr + jnp.left_shift(jnp.int32(1), jnp.int32(30) - i)
        return jnp.where(count_ge(cand) >= kf, cand, thr)

    return lax.fori_loop(0, 31, body, thr)


def _sortable_key(scores):
    bits = lax.bitcast_convert_type(scores + 0.0, jnp.int32)
    return jnp.where(bits < 0, bits ^ jnp.int32(0x7FFFFFFF), bits)


def _rmsnorm_body(x_ref, g_ref, o_ref):
    x = x_ref[...]
    ms = jnp.mean(x * x, axis=-1, keepdims=True)
    o_ref[...] = ((x * lax.rsqrt(ms + EPS)) * g_ref[...]).astype(o_ref.dtype)


def _rmsnorm(x, g, out_dtype, bm):
    m, d = x.shape
    bm = min(bm, m)
    return pl.pallas_call(
        _rmsnorm_body,
        grid=(m // bm,),
        in_specs=[pl.BlockSpec((bm, d), lambda i: (i, 0)), pl.BlockSpec((1, d), lambda i: (0, 0))],
        out_specs=pl.BlockSpec((bm, d), lambda i: (i, 0)),
        out_shape=jax.ShapeDtypeStruct((m, d), out_dtype),
        compiler_params=_params(1, 40),
        name="rmsnorm",
    )(x, g.reshape(1, d))


def _mm_body(x_ref, w_ref, o_ref):
    o_ref[...] = jnp.dot(x_ref[...], w_ref[...], preferred_element_type=F32)


def _mm_nt_body(x_ref, wt_ref, o_ref):
    o_ref[...] = _dot_nt(x_ref[...], wt_ref[...])


def _matmul_nt(x, wt, bm, bn, name):
    m, k = x.shape
    n = wt.shape[0]
    bm, bn = min(bm, m), min(bn, n)
    return pl.pallas_call(
        _mm_nt_body,
        grid=(m // bm, n // bn),
        in_specs=[pl.BlockSpec((bm, k), lambda i, j: (i, 0)), pl.BlockSpec((bn, k), lambda i, j: (j, 0))],
        out_specs=pl.BlockSpec((bm, bn), lambda i, j: (i, j)),
        out_shape=jax.ShapeDtypeStruct((m, n), F32),
        compiler_params=_params(2, 48),
        name=name,
    )(x, wt)


def _matmul(x, w, bm, bn, name):
    m, k = x.shape
    n = w.shape[1]
    bm, bn = min(bm, m), min(bn, n)
    return pl.pallas_call(
        _mm_body,
        grid=(m // bm, n // bn),
        in_specs=[pl.BlockSpec((bm, k), lambda i, j: (i, 0)), pl.BlockSpec((k, bn), lambda i, j: (0, j))],
        out_specs=pl.BlockSpec((bm, bn), lambda i, j: (i, j)),
        out_shape=jax.ShapeDtypeStruct((m, n), F32),
        compiler_params=_params(2, 48),
        name=name,
    )(x, w)


def _wprep_body(shifts, tab_ref, a_ref, b_ref, o_ref):
    shift = tab_ref[1, pl.program_id(1)]

    @pl.when(shift == 0)
    def _():
        o_ref[...] = a_ref[...].astype(o_ref.dtype)

    for sh in shifts:
        @pl.when(shift == sh)
        def _(sh=sh):
            x = jnp.concatenate([a_ref[...], b_ref[...]], axis=0)
            o_ref[...] = pltpu.roll(x, 2 * COL_TILE - sh, 0)[:COL_TILE, :].astype(o_ref.dtype)


def _wprep(wt, tiles, bk):
    n_src, k = wt.shape
    last = pl.cdiv(n_src, COL_TILE) - 1
    tab = jnp.asarray([[c // COL_TILE for c in tiles], [c % COL_TILE for c in tiles]], jnp.int32)
    shifts = tuple(sorted({c % COL_TILE for c in tiles} - {0}))
    bk = min(bk, k)
    grid_spec = pltpu.PrefetchScalarGridSpec(
        num_scalar_prefetch=1,
        grid=(k // bk, len(tiles)),
        in_specs=[pl.BlockSpec((COL_TILE, bk), lambda i, t, tab: (tab[0, t], i)),
                  pl.BlockSpec((COL_TILE, bk), lambda i, t, tab: (jnp.minimum(tab[0, t] + 1, last), i))],
        out_specs=pl.BlockSpec((COL_TILE, bk), lambda i, t, tab: (t, i)),
    )
    return pl.pallas_call(
        functools.partial(_wprep_body, shifts),
        grid_spec=grid_spec,
        out_shape=jax.ShapeDtypeStruct((COL_TILE * len(tiles), k), BF16),
        compiler_params=_params(2, 40),
        name="w_prep",
    )(tab, wt, wt)


def _cast_body(x_ref, o_ref):
    o_ref[...] = x_ref[...].astype(o_ref.dtype)


def _cast_bf16(w, bm):
    m, n = w.shape
    return pl.pallas_call(
        _cast_body,
        grid=(m // bm,),
        in_specs=[pl.BlockSpec((bm, n), lambda i: (i, 0))],
        out_specs=pl.BlockSpec((bm, n), lambda i: (i, 0)),
        out_shape=jax.ShapeDtypeStruct((m, n), BF16),
        compiler_params=_params(1, 40),
        name="w_cast",
    )(w)


def _post_body(ka_ref, s1_ref, s2_ref, tab_ref, bf_ref, ka_o, sm_o):
    c, sa, sb = tab_ref[0], tab_ref[1], tab_ref[2]
    for h in range(KVH_A):
        sl = slice(h * HEAD_DIM, (h + 1) * HEAD_DIM)
        ka_o[:, sl] = _rot(ka_ref[:, sl], c, sa, sb, HEAD_DIM // ROT_FRAC // 2)
    r = _rot(s1_ref[...], tab_ref[6], tab_ref[7], tab_ref[8], D_IDX // ROT_FRAC // 2)
    lf = _log_sigmoid(pltpu.roll(s2_ref[...], LF_LANE, 1) + bf_ref[...])
    lane = lax.broadcasted_iota(jnp.int32, r.shape, 1)
    sm_o[...] = jnp.where(lane < LF_LANE, r, jnp.where(lane < LF_LANE + H_B, lf, 0.0))


def _post(proj, tab, bf_row, seq_rows, bm):
    off = _layout()[0]
    m = proj.shape[0]
    bm = min(bm, m)
    tb = seq_rows // bm
    return pl.pallas_call(
        _post_body,
        grid=(m // bm,),
        in_specs=[pl.BlockSpec((bm, W_KV), lambda i: (i, off["ka"] // W_KV)),
                  pl.BlockSpec((bm, LANES), lambda i: (i, off["s1"] // LANES)),
                  pl.BlockSpec((bm, LANES), lambda i: (i, off["s2"] // LANES)),
                  pl.BlockSpec((9, bm, LANES), lambda i: (0, i % tb, 0)),
                  pl.BlockSpec((1, LANES), lambda i: (0, 0))],
        out_specs=[pl.BlockSpec((bm, W_KV), lambda i: (i, 0)), pl.BlockSpec((bm, LANES), lambda i: (i, 0))],
        out_shape=[jax.ShapeDtypeStruct((m, W_KV), F32), jax.ShapeDtypeStruct((m, LANES), F32)],
        compiler_params=_params(1, 40),
        name="kv_post",
    )(proj, proj, proj, tab, bf_row)


def _qrot_body(qa_ref, iq_ref, tab_ref, qa_o, iq_o):
    for h in range(H_A):
        sl = slice(h * HEAD_DIM, (h + 1) * HEAD_DIM)
        qa_o[:, sl] = _rot(qa_ref[:, sl], tab_ref[0], tab_ref[1], tab_ref[2], HEAD_DIM // ROT_FRAC // 2)
    for hp in range(H_IDX * D_IDX // LANES):
        sl = slice(hp * LANES, (hp + 1) * LANES)
        iq_o[:, sl] = _rot(iq_ref[:, sl], tab_ref[3], tab_ref[4], tab_ref[5], D_IDX // ROT_FRAC // 2)


def _qrot(proj, tab):
    off = _layout()[0]
    m = proj.shape[0]
    wi = H_IDX * D_IDX
    return pl.pallas_call(
        _qrot_body,
        grid=(1,),
        in_specs=[pl.BlockSpec((m, W_A), lambda i: (0, off["qa"] // W_A)),
                  pl.BlockSpec((m, wi), lambda i: (0, off["iq"] // wi)),
                  pl.BlockSpec((9, m, LANES), lambda i: (0, 0, 0))],
        out_specs=[pl.BlockSpec((m, W_A), lambda i: (0, 0)), pl.BlockSpec((m, wi), lambda i: (0, 0))],
        out_shape=[jax.ShapeDtypeStruct((m, W_A), F32), jax.ShapeDtypeStruct((m, wi), F32)],
        compiler_params=_params(1, 32),
        name="q_rotary",
    )(proj, proj, tab)


def _cumsum_body(sm_ref, ccol_ref, crow_ref):
    r = lax.broadcasted_iota(jnp.int32, (LANES, LANES), 0)
    c = lax.broadcasted_iota(jnp.int32, (LANES, LANES), 1)
    tri = (c <= r).astype(BF16)
    keep = (c >= LF_LANE) & (c < LF_LANE + H_B)
    carry = jnp.zeros((1, LANES), F32)
    for blk in range(ccol_ref.shape[0] // LANES):
        rows = slice(blk * LANES, (blk + 1) * LANES)
        x = jnp.where(keep, sm_ref[rows, :], 0.0)
        hi, mid, lo = _split3(x)
        cb = (jnp.dot(tri, hi, preferred_element_type=F32) + jnp.dot(tri, mid, preferred_element_type=F32)
              + jnp.dot(tri, lo, preferred_element_type=F32)) + carry
        carry = cb[LANES - 1:LANES, :]
        ccol_ref[rows, :] = cb
        crow_ref[0, :, rows] = cb.T


def _cumsum(small, batch, seq):
    return pl.pallas_call(
        _cumsum_body,
        grid=(batch,),
        in_specs=[pl.BlockSpec((seq, LANES), lambda b: (b, 0))],
        out_specs=[pl.BlockSpec((seq, LANES), lambda b: (b, 0)), pl.BlockSpec((1, LANES, seq), lambda b: (b, 0, 0))],
        out_shape=[jax.ShapeDtypeStruct((batch * seq, LANES), F32), jax.ShapeDtypeStruct((batch, LANES, seq), F32)],
        compiler_params=_params(1, 32),
        name="fox_cumsum",
    )(small)


def _softmax_av(s, v):
    m = jnp.max(s, axis=1, keepdims=True)
    p = jnp.exp(s - m)
    l = jnp.sum(p, axis=1, keepdims=True)
    return jnp.dot(p.astype(BF16), v, preferred_element_type=F32) / l


def _dsa_prompt_body(topk, q_ref, iq_ref, sm_ref, z_ref, k_ref, v_ref, ik_ref, tab_ref, u_ref):
    tq = q_ref.shape[0]
    s_len = k_ref.shape[0]
    row0 = pl.program_id(1) * tq
    rows = row0 + lax.broadcasted_iota(jnp.int32, (tq, s_len), 0)
    cols = lax.broadcasted_iota(jnp.int32, (tq, s_len), 1)
    causal = cols <= rows

    acc = jnp.zeros((tq, s_len), F32)
    for hp in range(H_IDX * D_IDX // LANES):
        x = iq_ref[:, hp * LANES:(hp + 1) * LANES]
        xb = _rot(x, tab_ref[3], tab_ref[4], tab_ref[5], D_IDX // ROT_FRAC // 2).astype(BF16)
        for j in range(LANES // D_IDX):
            h = hp * (LANES // D_IDX) + j
            s = _dot_nt(xb, ik_ref[:, j * LANES:(j + 1) * LANES]) * (D_IDX ** -0.5)
            w = sm_ref[:, D_IDX + h:D_IDX + h + 1] * (H_IDX ** -0.5)
            acc = acc + jnp.maximum(s, 0.0) * w
    key = _sortable_key(jnp.where(causal, acc, -jnp.inf))
    sel = (key >= _kth_largest_key(key, topk)) & causal
    bias = jnp.where(sel, 0.0, NEG)

    grp = H_A // KVH_A
    for h in range(H_A):
        sl = slice(h * HEAD_DIM, (h + 1) * HEAD_DIM)
        g = h // grp
        gs = slice(g * HEAD_DIM, (g + 1) * HEAD_DIM)
        qh = _rot(q_ref[:, sl], tab_ref[0], tab_ref[1], tab_ref[2], HEAD_DIM // ROT_FRAC // 2).astype(BF16)
        s = _dot_nt(qh, k_ref[:, gs]) * (HEAD_DIM ** -0.5) + bias
        o = _softmax_av(s, v_ref[:, gs])
        u_ref[:, sl] = (o * _silu(z_ref[:, sl])).astype(u_ref.dtype)


def _dsa_prompt(proj, small, ka_bf, va_bf, ik_bf, tab, batch, seq, tq):
    off = _layout()[0]
    nq = seq // tq
    wi = H_IDX * D_IDX
    topk = min(TOPK_MAX, seq // 4)
    row = lambda b, q: b * nq + q
    return pl.pallas_call(
        functools.partial(_dsa_prompt_body, topk),
        grid=(batch, nq),
        in_specs=[pl.BlockSpec((tq, W_A), lambda b, q: (row(b, q), off["qa"] // W_A)),
                  pl.BlockSpec((tq, wi), lambda b, q: (row(b, q), off["iq"] // wi)),
                  pl.BlockSpec((tq, LANES), lambda b, q: (row(b, q), 0)),
                  pl.BlockSpec((tq, W_A), lambda b, q: (row(b, q), off["za"] // W_A)),
                  pl.BlockSpec((seq, W_KV), lambda b, q: (b, 0)),
                  pl.BlockSpec((seq, W_KV), lambda b, q: (b, 0)),
                  pl.BlockSpec((seq, 2 * LANES), lambda b, q: (b, 0)),
                  pl.BlockSpec((9, tq, LANES), lambda b, q: (0, q, 0))],
        out_specs=pl.BlockSpec((tq, W_A), lambda b, q: (row(b, q), 0)),
        out_shape=jax.ShapeDtypeStruct((batch * seq, W_A), BF16),
        compiler_params=_params(2, 56),
        name="dsa_prompt",
    )(proj, proj, small, proj, ka_bf, va_bf, ik_bf, tab)


def _fox_prompt_body(q_ref, z_ref, k_ref, v_ref, ccol_ref, crow_ref, u_ref):
    tq = q_ref.shape[0]
    s_len = k_ref.shape[0]
    row0 = pl.program_id(1) * tq
    rows = row0 + lax.broadcasted_iota(jnp.int32, (tq, s_len), 0)
    cols = lax.broadcasted_iota(jnp.int32, (tq, s_len), 1)
    bias = jnp.where(cols <= rows, 0.0, NEG)
    grp = H_B // KVH_B
    for h in range(H_B):
        sl = slice(h * HEAD_DIM, (h + 1) * HEAD_DIM)
        g = h // grp
        gs = slice(g * HEAD_DIM, (g + 1) * HEAD_DIM)
        s = _dot_nt(q_ref[:, sl].astype(BF16), k_ref[:, gs]) * (HEAD_DIM ** -0.5)
        s = s + ccol_ref[:, LF_LANE + h:LF_LANE + h + 1] - crow_ref[0, LF_LANE + h:LF_LANE + h + 1, :]
        o = _softmax_av(s + bias, v_ref[:, gs])
        u_ref[:, sl] = (o * _silu(z_ref[:, sl])).astype(u_ref.dtype)


def _fox_prompt(proj, kb_bf, vb_bf, ccol, crow, batch, seq, tq):
    off = _layout()[0]
    nq = seq // tq
    row = lambda b, q: b * nq + q
    return pl.pallas_call(
        _fox_prompt_body,
        grid=(batch, nq),
        in_specs=[pl.BlockSpec((tq, W_B), lambda b, q: (row(b, q), off["qb"] // W_B)),
                  pl.BlockSpec((tq, W_B), lambda b, q: (row(b, q), off["zb"] // W_B)),
                  pl.BlockSpec((seq, W_KV), lambda b, q: (b, 0)),
                  pl.BlockSpec((seq, W_KV), lambda b, q: (b, 0)),
                  pl.BlockSpec((tq, LANES), lambda b, q: (row(b, q), 0)),
                  pl.BlockSpec((1, LANES, seq), lambda b, q: (b, 0, 0))],
        out_specs=pl.BlockSpec((tq, W_B), lambda b, q: (row(b, q), 0)),
        out_shape=jax.ShapeDtypeStruct((batch * seq, W_B), BF16),
        compiler_params=_params(2, 56),
        name="fox_prompt",
    )(proj, proj, kb_bf, vb_bf, ccol, crow)


def _cross_prompt_body(q_ref, z_ref, k_ref, v_ref, u_ref):
    for h in range(H_C):
        sl = slice(h * HD_C, (h + 1) * HD_C)
        s = _dot_nt(q_ref[:, sl].astype(BF16), k_ref[:, sl]) * (HD_C ** -0.5)
        o = _softmax_av(s, v_ref[:, sl])
        u_ref[:, sl] = (o * _silu(z_ref[:, sl])).astype(u_ref.dtype)


def _cross_prompt(proj, mk_bf, mv_bf, batch, seq, tq):
    off = _layout()[0]
    nq = seq // tq
    row = lambda b, q: b * nq + q
    return pl.pallas_call(
        _cross_prompt_body,
        grid=(batch, nq),
        in_specs=[pl.BlockSpec((tq, W_C), lambda b, q: (row(b, q), off["qc"] // W_C)),
                  pl.BlockSpec((tq, W_C), lambda b, q: (row(b, q), off["zc"] // W_C)),
                  pl.BlockSpec((N_MEM, W_C), lambda b, q: (b, 0)),
                  pl.BlockSpec((N_MEM, W_C), lambda b, q: (b, 0))],
        out_specs=pl.BlockSpec((tq, W_C), lambda b, q: (row(b, q), 0)),
        out_shape=jax.ShapeDtypeStruct((batch * seq, W_C), BF16),
        compiler_params=_params(2, 40),
        name="cross_prompt",
    )(proj, proj, mk_bf, mv_bf)


def _merge_body(ua_ref, ub_ref, uc_ref, wa_ref, wb_ref, wc_ref, ga_ref, gb_ref, gc_ref, h_ref):
    h = (_sigmoid(ga_ref[...]) * jnp.dot(ua_ref[...], wa_ref[...], preferred_element_type=F32)
         + _sigmoid(gb_ref[...]) * jnp.dot(ub_ref[...], wb_ref[...], preferred_element_type=F32)
         + _sigmoid(gc_ref[...]) * jnp.dot(uc_ref[...], wc_ref[...], preferred_element_type=F32))
    h_ref[...] = h.astype(h_ref.dtype)


def _merge(ua, ub, uc, wa, wb, wc, proj, bm, bn):
    off = _layout()[0]
    m = ua.shape[0]
    d = wa.shape[1]
    bm, bn = min(bm, m), min(bn, d)
    g0 = off["gates"] // bn
    nd = d // bn
    return pl.pallas_call(
        _merge_body,
        grid=(m // bm, nd),
        in_specs=[pl.BlockSpec((bm, W_A), lambda i, j: (i, 0)),
                  pl.BlockSpec((bm, W_B), lambda i, j: (i, 0)),
                  pl.BlockSpec((bm, W_C), lambda i, j: (i, 0)),
                  pl.BlockSpec((W_A, bn), lambda i, j: (0, j)),
                  pl.BlockSpec((W_B, bn), lambda i, j: (0, j)),
                  pl.BlockSpec((W_C, bn), lambda i, j: (0, j)),
                  pl.BlockSpec((bm, bn), lambda i, j: (i, g0 + j)),
                  pl.BlockSpec((bm, bn), lambda i, j: (i, g0 + nd + j)),
                  pl.BlockSpec((bm, bn), lambda i, j: (i, g0 + 2 * nd + j))],
        out_specs=pl.BlockSpec((bm, bn), lambda i, j: (i, j)),
        out_shape=jax.ShapeDtypeStruct((m, d), BF16),
        compiler_params=_params(2, 48),
        name="merge",
    )(ua, ub, uc, wa, wb, wc, proj, proj, proj)


def _outproj_body(x_ref, h_ref, w_ref, y_ref):
    y_ref[...] = x_ref[...] + jnp.dot(h_ref[...], w_ref[...], preferred_element_type=F32)


def _outproj(x, h, w, bm, bn):
    m, d = x.shape
    bm, bn = min(bm, m), min(bn, d)
    return pl.pallas_call(
        _outproj_body,
        grid=(m // bm, d // bn),
        in_specs=[pl.BlockSpec((bm, bn), lambda i, j: (i, j)),
                  pl.BlockSpec((bm, d), lambda i, j: (i, 0)),
                  pl.BlockSpec((d, bn), lambda i, j: (0, j))],
        out_specs=pl.BlockSpec((bm, bn), lambda i, j: (i, j)),
        out_shape=jax.ShapeDtypeStruct((m, d), F32),
        compiler_params=_params(2, 48),
        name="out_proj",
    )(x, h, w)


def _idx_sample_body(pps, pt_ref, iq_ref, w_ref, ikn_ref, *rest):
    pages = rest[:pps]
    sc_ref, own_ref = rest[pps], rest[pps + 1]
    iq = iq_ref[0]
    w = w_ref[0] * (H_IDX ** -0.5)
    for i in range(pps):
        s = jnp.dot(iq, pages[i][...].astype(BF16), preferred_element_type=F32) * (D_IDX ** -0.5)
        sc_ref[0, :, i * PAGE_SIZE:(i + 1) * PAGE_SIZE] = jnp.sum(jnp.maximum(s, 0.0) * w, axis=0, keepdims=True)

    @pl.when(pl.program_id(1) == 0)
    def _():
        kn = ikn_ref[0].astype(F32)
        s = jnp.sum(iq.astype(F32) * kn, axis=1, keepdims=True) * (D_IDX ** -0.5)
        own = jnp.sum(jnp.maximum(s, 0.0) * w, axis=0, keepdims=True)
        lane = lax.broadcasted_iota(jnp.int32, (1, LANES), 1)
        own_ref[0] = jnp.where(lane == 0, own, -jnp.inf)


def _idx_sample(page_table, iq_bf, iw, ik_new_bf, pool_idx, pps):
    b, n_pages = page_table.shape
    npc = n_pages // pps
    page_specs = [pl.BlockSpec((None, None, D_IDX, PAGE_SIZE),
                               lambda bi, pc, pt, i=i: (0, pt[bi, pc * pps + i], 0, 0)) for i in range(pps)]
    grid_spec = pltpu.PrefetchScalarGridSpec(
        num_scalar_prefetch=1,
        grid=(b, npc),
        in_specs=[pl.BlockSpec((1, H_IDX, D_IDX), lambda bi, pc, pt: (bi, 0, 0)),
                  pl.BlockSpec((1, H_IDX, 1), lambda bi, pc, pt: (bi, 0, 0)),
                  pl.BlockSpec((1, 1, D_IDX), lambda bi, pc, pt: (bi, 0, 0))] + page_specs,
        out_specs=[pl.BlockSpec((1, 1, pps * PAGE_SIZE), lambda bi, pc, pt: (bi, 0, pc)),
                   pl.BlockSpec((1, 1, LANES), lambda bi, pc, pt: (bi, 0, 0))],
    )
    return pl.pallas_call(
        functools.partial(_idx_sample_body, pps),
        grid_spec=grid_spec,
        out_shape=[jax.ShapeDtypeStruct((b, 1, n_pages * PAGE_SIZE), F32),
                   jax.ShapeDtypeStruct((b, 1, LANES), F32)],
        compiler_params=_params(2, 32),
        name="idx_sample",
    )(page_table, iq_bf, iw, ik_new_bf, *([pool_idx] * pps))


def _topk_sample_body(topk, n_valid, grp, sc_ref, bias_ref):
    sc = sc_ref[...]
    lane = lax.broadcasted_iota(jnp.int32, sc.shape, 1)
    valid = lane < n_valid
    key = _sortable_key(jnp.where(valid, sc, -jnp.inf))
    sel = jnp.where((key >= _kth_largest_key(key, topk)) & valid, 1.0, 0.0).astype(BF16)
    k = lax.broadcasted_iota(jnp.int32, (LANES, LANES * grp), 0)
    r = lax.broadcasted_iota(jnp.int32, (LANES, LANES * grp), 1)
    rep = ((r >= k * grp) & (r < (k + 1) * grp)).astype(BF16)
    for c in range(sc.shape[1] // LANES):
        flags = jnp.dot(sel[:, c * LANES:(c + 1) * LANES], rep, preferred_element_type=F32)
        bias_ref[:, c * LANES * grp:(c + 1) * LANES * grp] = jnp.where(flags > 0.5, 0.0, NEG)


def _topk_sample(scores, n_valid, grp):
    topk = min(TOPK_MAX, n_valid // 4)
    b, n = scores.shape
    return pl.pallas_call(
        functools.partial(_topk_sample_body, topk, n_valid, grp),
        grid=(1,),
        in_specs=[pl.BlockSpec((b, n), lambda i: (0, 0))],
        out_specs=pl.BlockSpec((b, n * grp), lambda i: (0, 0)),
        out_shape=jax.ShapeDtypeStruct((b, n * grp), F32),
        compiler_params=_params(1, 40),
        name="topk_sample",
    )(scores)


def _fox_bias_body(pps, grp, pt_ref, lfn_ref, *rest):
    pages = rest[:pps]
    d_ref, x_ref, carry_ref = rest[pps], rest[pps + 1], rest[pps + 2]
    rows = PAGE_SIZE * grp
    j = lax.broadcasted_iota(jnp.int32, (LANES, rows), 0)
    r = lax.broadcasted_iota(jnp.int32, (LANES, rows), 1)
    upper = (r < j * grp).astype(BF16)

    @pl.when(pl.program_id(1) == 0)
    def _():
        x_ref[...] = jnp.zeros_like(x_ref)
        x_ref[0:1, :] = lfn_ref[0]
        carry_ref[...] = x_ref[...].T[0:HEAD_ROWS, 0:1]
        x_ref[0:1, :] = jnp.zeros((1, LANES), F32)

    for i in reversed(range(pps)):
        x_ref[:, 0:H_B] = pages[i][...]
        xt = x_ref[...].T[0:HEAD_ROWS, :]
        hi, mid, lo = _split3(xt)
        d_loc = (jnp.dot(hi, upper, preferred_element_type=F32) + jnp.dot(mid, upper, preferred_element_type=F32)
                 + jnp.dot(lo, upper, preferred_element_type=F32))
        d_ref[0, :, i * rows:(i + 1) * rows] = d_loc + carry_ref[...]
        carry_ref[...] = carry_ref[...] + jnp.sum(xt, axis=1, keepdims=True)


def _fox_bias(page_table, lf_new, pool_lf, pps, grp):
    b, n_pages = page_table.shape
    npc = n_pages // pps
    rows = PAGE_SIZE * grp
    page_specs = [pl.BlockSpec((None, None, PAGE_SIZE, H_B),
                               lambda bi, pc, pt, i=i: (0, pt[bi, (npc - 1 - pc) * pps + i], 0, 0))
                  for i in range(pps)]
    grid_spec = pltpu.PrefetchScalarGridSpec(
        num_scalar_prefetch=1,
        grid=(b, npc),
        in_specs=[pl.BlockSpec((1, 1, LANES), lambda bi, pc, pt: (bi, 0, 0))] + page_specs,
        out_specs=pl.BlockSpec((1, HEAD_ROWS, pps * rows), lambda bi, pc, pt: (bi, 0, npc - 1 - pc)),
        scratch_shapes=[pltpu.VMEM((LANES, LANES), F32), pltpu.VMEM((HEAD_ROWS, 1), F32)],
    )
    return pl.pallas_call(
        functools.partial(_fox_bias_body, pps, grp),
        grid_spec=grid_spec,
        out_shape=jax.ShapeDtypeStruct((b, HEAD_ROWS, n_pages * rows), F32),
        compiler_params=_params(2, 32),
        name="fox_bias_sample",
    )(page_table, lf_new, *([pool_lf] * pps))


def _paged_body(pps, scale, pt_ref, q_ref, bias_ref, gm_ref, kn_ref, vn_ref, nb_ref, z_ref, *rest):
    kps, vps = rest[:pps], rest[pps:2 * pps]
    u_ref, m_ref, l_ref, acc_ref = rest[2 * pps:2 * pps + 4]
    pc = pl.program_id(1)
    rows = kps[0].shape[0]
    q = q_ref[0]

    @pl.when(pc == 0)
    def _():
        kn = kn_ref[0].astype(BF16).astype(F32)
        vn = vn_ref[0].astype(BF16).astype(F32)
        nb = nb_ref[0]
        on = nb > 0.5 * NEG
        s0 = jnp.sum(q.astype(F32) * kn, axis=1, keepdims=True) * scale + nb
        m_ref[...] = jnp.where(on, s0, NEG)
        l_ref[...] = jnp.where(on, 1.0, 0.0)
        acc_ref[...] = jnp.where(on, vn, 0.0)

    ss = [(_dot_nt(q, kps[i][...].astype(BF16)) * scale + bias_ref[0, :, i * rows:(i + 1) * rows]) + gm_ref[...]
          for i in range(pps)]
    m_old = m_ref[...]
    m_new = m_old
    for s in ss:
        m_new = jnp.maximum(m_new, jnp.max(s, axis=1, keepdims=True))
    alpha = jnp.exp(m_old - m_new)
    l_new = alpha * l_ref[...]
    acc = alpha * acc_ref[...]
    for i, s in enumerate(ss):
        pf = jnp.where(s > 0.5 * NEG, jnp.exp(s - m_new), 0.0)
        l_new = l_new + jnp.sum(pf, axis=1, keepdims=True)
        acc = acc + jnp.dot(pf.astype(BF16), vps[i][...].astype(BF16), preferred_element_type=F32)
    l_ref[...] = l_new
    acc_ref[...] = acc
    m_ref[...] = m_new

    @pl.when(pc == pl.num_programs(1) - 1)
    def _():
        u_ref[0] = (acc_ref[...] / l_ref[...]) * _silu(z_ref[0])


def _paged_attn(page_idx, q, bias, gmask, k_new, v_new, new_bias, z, pool_k, pool_v, pps, scale, name, paged=True):
    b, n_pages = page_idx.shape
    rows, hd = (pool_k.shape[1] if paged else pool_k.shape[1] // n_pages), pool_k.shape[2]
    rb = bias.shape[1]
    npc = n_pages // pps
    if paged:
        kspecs = [pl.BlockSpec((None, rows, hd), lambda bi, pc, pt, i=i: (pt[bi, pc * pps + i], 0, 0))
                  for i in range(pps)]
    else:
        kspecs = [pl.BlockSpec((None, rows, hd), lambda bi, pc, pt, i=i: (bi, pc * pps + i, 0)) for i in range(pps)]
    per_b = lambda bi, pc, pt: (bi, 0, 0)
    head_blk = pl.BlockSpec((1, HEAD_ROWS, hd), per_b)
    grid_spec = pltpu.PrefetchScalarGridSpec(
        num_scalar_prefetch=1,
        grid=(b, npc),
        in_specs=[head_blk,
                  pl.BlockSpec((1, rb, pps * rows), lambda bi, pc, pt: (bi, 0, pc)),
                  pl.BlockSpec((HEAD_ROWS, rows), lambda bi, pc, pt: (0, 0)),
                  head_blk, head_blk,
                  pl.BlockSpec((1, HEAD_ROWS, 1), per_b),
                  head_blk] + kspecs + kspecs,
        out_specs=head_blk,
        scratch_shapes=[pltpu.VMEM((HEAD_ROWS, 1), F32), pltpu.VMEM((HEAD_ROWS, 1), F32),
                        pltpu.VMEM((HEAD_ROWS, hd), F32)],
    )
    return pl.pallas_call(
        functools.partial(_paged_body, pps, scale),
        grid_spec=grid_spec,
        out_shape=jax.ShapeDtypeStruct((b, HEAD_ROWS, hd), F32),
        compiler_params=_params(2, 48),
        name=name,
    )(page_idx, q, bias, gmask, k_new, v_new, new_bias, z, *([pool_k] * pps), *([pool_v] * pps))


def _rotary_tables(pos):
    def base(width):
        rot = width // ROT_FRAC
        half = rot // 2
        inv = jnp.power(jnp.float32(ROPE_THETA), -jnp.arange(half, dtype=F32) * (2.0 / rot))
        ang = pos.astype(F32)[:, None] * inv[None, :]
        cos, sin = jnp.cos(ang), jnp.sin(ang)
        n = pos.shape[0]
        ones, zeros = jnp.ones((n, width - rot), F32), jnp.zeros((n, width - rot), F32)
        zh = jnp.zeros((n, half), F32)
        return (jnp.concatenate([cos, cos, ones], 1), jnp.concatenate([-sin, zh, zeros], 1),
                jnp.concatenate([zh, sin, zeros], 1))

    c1, a1, b1 = base(HEAD_DIM)
    c2, a2, b2 = base(D_IDX)
    n = pos.shape[0]
    pad1, pad0 = jnp.ones((n, LANES - D_IDX), F32), jnp.zeros((n, LANES - D_IDX), F32)
    return jnp.stack([c1, a1, b1,
                      jnp.concatenate([c2, c2], 1), jnp.concatenate([a2, a2], 1), jnp.concatenate([b2, b2], 1),
                      jnp.concatenate([c2, pad1], 1), jnp.concatenate([a2, pad0], 1), jnp.concatenate([b2, pad0], 1)])


def _head_rows(x, n_heads, hd):
    b = x.shape[0]
    return jnp.pad(x.reshape(b, n_heads, hd), ((0, 0), (0, HEAD_ROWS - n_heads), (0, 0)))


def _group_rows(x, n_heads, n_groups, hd):
    b = x.shape[0]
    gid = np.minimum(np.arange(HEAD_ROWS), n_heads - 1) // (n_heads // n_groups)
    return x.reshape(b, n_groups, hd)[:, gid]


def _group_mask(n_heads, n_groups):
    gid = np.minimum(np.arange(HEAD_ROWS), n_heads - 1) // (n_heads // n_groups)
    r = np.arange(PAGE_SIZE * n_groups)
    return jnp.asarray(np.where((r[None, :] % n_groups) == gid[:, None], 0.0, NEG), F32)


def _trunk(x2d, proj, ua, ub, uc, w_a, w_b, w_c, w_o, g_final):
    h = _merge(ua, ub, uc, w_a, w_b, w_c, proj, 1024, 512)
    y = _outproj(x2d, h, w_o, 1024, 512)
    return _rmsnorm(y, g_final, F32, 256)


def kernel(x_prompt, x_sample, cache_a_k, cache_a_v, cache_a_idx, cache_b_k, cache_b_v, cache_b_logf, cache_mem_k,
           cache_mem_v, page_table, mem_prompt, g_norm, w_in, b_forget, w_br_a, w_br_b, w_br_c, w_out, g_mem,
           w_mem_kv, g_final):
    off, _, tiles = _layout()
    bsz, seq, d = x_prompt.shape
    dec = x_sample.shape[0]
    tq = min(256, seq)

    w_in_t = _wprep(jnp.transpose(w_in[0]), tiles, 2048)
    w_a, w_b, w_c, w_o, w_m = (_cast_bf16(w[0], 256) for w in (w_br_a, w_br_b, w_br_c, w_out, w_mem_kv))
    bf_row = jnp.zeros((1, LANES), F32).at[0, LF_LANE:LF_LANE + H_B].set(b_forget[0])

    xp = x_prompt.reshape(bsz * seq, d)
    proj = _matmul_nt(_rmsnorm(xp, g_norm[0], BF16, 256), w_in_t, 1024, 512, "in_proj")
    tab_p = _rotary_tables(jnp.arange(seq, dtype=jnp.int32))
    ka_rot, small = _post(proj, tab_p, bf_row, seq, 512)
    col = lambda name, width: proj[:, off[name]:off[name] + width]
    va, kb, vb = col("va", W_KV), col("kb", W_KV), col("vb", W_KV)
    ik = small[:, :D_IDX]
    logf = small[:, LF_LANE:LF_LANE + H_B]

    mem_n = _rmsnorm(mem_prompt.reshape(bsz * N_MEM, d), g_mem[0], BF16, 256)
    mkv = _matmul(mem_n, w_m, 1024, 1024, "mem_kv")
    mk, mv = mkv[:, :W_C], mkv[:, W_C:]

    ik_bf = ik.astype(BF16)
    ik_pad = jnp.zeros_like(ik_bf)
    ik2 = jnp.concatenate([ik_bf, ik_pad, ik_pad, ik_bf], axis=1)
    ua = _dsa_prompt(proj, small, ka_rot.astype(BF16), va.astype(BF16), ik2, tab_p, bsz, seq, tq)
    ccol, crow = _cumsum(small, bsz, seq)
    ub = _fox_prompt(proj, kb.astype(BF16), vb.astype(BF16), ccol, crow, bsz, seq, tq)
    uc = _cross_prompt(proj, mk.astype(BF16), mv.astype(BF16), bsz, seq, tq)
    y_prompt = _trunk(xp, proj, ua, ub, uc, w_a, w_b, w_c, w_o, g_final).reshape(bsz, seq, d)

    xs = x_sample.reshape(dec, d)
    proj_s = _matmul_nt(_rmsnorm(xs, g_norm[0], BF16, dec), w_in_t, dec, 512, "in_proj_sample")
    tab_s = _rotary_tables(jnp.full((dec,), page_table.shape[1] * PAGE_SIZE, jnp.int32))
    ka_s, small_s = _post(proj_s, tab_s, bf_row, dec, dec)
    cols = lambda name, width: proj_s[:, off[name]:off[name] + width]
    va_s, kb_s, vb_s = cols("va", W_KV), cols("kb", W_KV), cols("vb", W_KV)
    ik_s = small_s[:, :D_IDX]
    iw_s = small_s[:, D_IDX:D_IDX + H_IDX]
    logf_s = small_s[:, LF_LANE:LF_LANE + H_B]
    qa_s, iq_s = _qrot(proj_s, tab_s)

    n_pages = page_table.shape[1]
    pps = min(16, n_pages)
    pps_small = min(32, n_pages)
    scores, own = _idx_sample(page_table, iq_s.astype(BF16).reshape(dec, H_IDX, D_IDX), iw_s.reshape(dec, H_IDX, 1),
                              ik_s.astype(BF16).reshape(dec, 1, D_IDX), jnp.transpose(cache_a_idx, (0, 1, 3, 2)),
                              pps_small)
    n_keys = n_pages * PAGE_SIZE
    n_pool = cache_a_k.shape[1]
    flat = lambda pool, g, hd: pool.reshape(n_pool, PAGE_SIZE * g, hd)
    sel_bias = _topk_sample(jnp.concatenate([scores[:, 0], own[:, 0]], axis=1), n_keys + 1, KVH_A)

    scale = HEAD_DIM ** -0.5
    ua_full = _paged_attn(
        page_table, _head_rows(qa_s, H_A, HEAD_DIM).astype(BF16), sel_bias[:, None, :n_keys * KVH_A],
        _group_mask(H_A, KVH_A), _group_rows(ka_s, H_A, KVH_A, HEAD_DIM), _group_rows(va_s, H_A, KVH_A, HEAD_DIM),
        jnp.broadcast_to(sel_bias[:, None, n_keys * KVH_A:n_keys * KVH_A + 1], (dec, HEAD_ROWS, 1)),
        _head_rows(cols("za", W_A), H_A, HEAD_DIM),
        flat(cache_a_k, KVH_A, HEAD_DIM), flat(cache_a_v, KVH_A, HEAD_DIM), pps, scale, "dsa_sample")

    lf_new = jnp.pad(logf_s, ((0, 0), (0, LANES - H_B))).reshape(dec, 1, LANES)
    d_bias = _fox_bias(page_table, lf_new, cache_b_logf, min(16, n_pages), KVH_B)
    ub_full = _paged_attn(
        page_table, _head_rows(cols("qb", W_B), H_B, HEAD_DIM).astype(BF16), d_bias,
        _group_mask(H_B, KVH_B), _group_rows(kb_s, H_B, KVH_B, HEAD_DIM), _group_rows(vb_s, H_B, KVH_B, HEAD_DIM),
        jnp.zeros((dec, HEAD_ROWS, 1), F32), _head_rows(cols("zb", W_B), H_B, HEAD_DIM),
        flat(cache_b_k, KVH_B, HEAD_DIM), flat(cache_b_v, KVH_B, HEAD_DIM), pps, scale, "fox_sample")

    mem_pages = N_MEM // PAGE_SIZE
    uc_full = _paged_attn(
        jnp.zeros((dec, mem_pages), jnp.int32), _head_rows(cols("qc", W_C), H_C, HD_C).astype(BF16),
        jnp.zeros((dec, 1, N_MEM * H_C), F32), _group_mask(H_C, H_C),
        jnp.zeros((dec, HEAD_ROWS, HD_C), F32), jnp.zeros((dec, HEAD_ROWS, HD_C), F32),
        jnp.full((dec, HEAD_ROWS, 1), NEG, F32), _head_rows(cols("zc", W_C), H_C, HD_C),
        cache_mem_k.reshape(dec, N_MEM * H_C, HD_C), cache_mem_v.reshape(dec, N_MEM * H_C, HD_C),
        mem_pages, HD_C ** -0.5, "cross_sample", paged=False)

    ua_s = ua_full[:, :H_A].reshape(dec, W_A).astype(BF16)
    ub_s = ub_full[:, :H_B].reshape(dec, W_B).astype(BF16)
    uc_s = uc_full[:, :H_C].reshape(dec, W_C).astype(BF16)
    y_sample = _trunk(xs, proj_s, ua_s, ub_s, uc_s, w_a, w_b, w_c, w_o, g_final).reshape(dec, 1, d)

    st = lambda a, b, t, *tail: a.reshape((1, b, t) + tail)
    return (y_prompt, y_sample,
            st(ka_rot, bsz, seq, KVH_A, HEAD_DIM), st(va, bsz, seq, KVH_A, HEAD_DIM), st(ik, bsz, seq, D_IDX),
            st(kb, bsz, seq, KVH_B, HEAD_DIM), st(vb, bsz, seq, KVH_B, HEAD_DIM), st(logf, bsz, seq, H_B),
            st(mk, bsz, N_MEM, H_C, HD_C), st(mv, bsz, N_MEM, H_C, HD_C),
            st(ka_s, dec, 1, KVH_A, HEAD_DIM), st(va_s, dec, 1, KVH_A, HEAD_DIM), st(ik_s, dec, 1, D_IDX),
            st(kb_s, dec, 1, KVH_B, HEAD_DIM), st(vb_s, dec, 1, KVH_B, HEAD_DIM), st(logf_s, dec, 1, H_B))
```

```python
import functools

import numpy as np
import jax
import jax.numpy as jnp
from jax import lax
from jax.experimental import pallas as pl
from jax.experimental.pallas import tpu as pltpu

D_MODEL = 4096
BATCH = 4
SEQ = 2048
DEC_BATCH = 32
PAST_LEN = 8192
PAGE_SIZE = 128
HEAD_DIM = 128
H_A = 12
KVH_A = 4
H_IDX = 16
D_IDX = 64
TOPK_MAX = 256
H_B = 12
KVH_B = 4
H_C = 4
HD_C = 256
N_MEM = 256
ROPE_THETA = 500000.0
ROT_FRAC = 4
EPS = 1e-6

F32 = jnp.float32
BF16 = jnp.bfloat16
NEG = -1e30
LANES = 128
HEAD_ROWS = 16
MIB = 1024 * 1024

W_A = H_A * HEAD_DIM
W_B = H_B * HEAD_DIM
W_C = H_C * HD_C
W_KV = KVH_A * HEAD_DIM
COL_TILE = 512
LF_LANE = D_IDX + H_IDX


def _layout():
    src, acc = {}, 0
    for n, s in zip(("qa", "ka", "va", "iq", "ik", "iw", "za", "qb", "kb", "vb", "fb", "zb", "qc", "zc", "gates"),
                    (W_A, W_KV, W_KV, H_IDX * D_IDX, D_IDX, H_IDX, W_A, W_B, W_KV, W_KV, H_B, W_B, W_C, W_C,
                     3 * D_MODEL)):
        src[n] = acc
        acc += s
    names = ("qa", "qb", "za", "zb", "ka", "va", "kb", "vb", "iq", "qc", "zc", "gates", "s1", "s2")
    sizes = (W_A, W_B, W_A, W_B, W_KV, W_KV, W_KV, W_KV, H_IDX * D_IDX, W_C, W_C, 3 * D_MODEL, COL_TILE, COL_TILE)
    starts = dict(src, s1=src["ik"], s2=src["fb"])
    off, tiles, acc = {}, [], 0
    for n, s in zip(names, sizes):
        off[n] = acc
        acc += s
        tiles += [starts[n] + COL_TILE * i for i in range(s // COL_TILE)]
    return off, acc, tiles


def _params(n_axes, vmem_mib):
    return pltpu.CompilerParams(dimension_semantics=("arbitrary",) * n_axes,
                                vmem_limit_bytes=vmem_mib * MIB)


def _sigmoid(z):
    return 1.0 / (1.0 + jnp.exp(-z))


def _silu(z):
    return z * _sigmoid(z)


def _log_sigmoid(z):
    return -(jnp.maximum(-z, 0.0) + jnp.log1p(jnp.exp(-jnp.abs(z))))


def _rot(x, c, sa, sb, half):
    n = x.shape[-1]
    return x * c + pltpu.roll(x, n - half, 1) * sa + pltpu.roll(x, half, 1) * sb


def _dot_nt(a, b):
    return lax.dot_general(a, b, (((1,), (1,)), ((), ())), preferred_element_type=F32)


def _split3(x):
    hi = x.astype(BF16)
    r = x - hi.astype(F32)
    mid = r.astype(BF16)
    lo = (r - mid.astype(F32)).astype(BF16)
    return hi, mid, lo


def _kth_largest_key(key, k):
    kf = jnp.float32(k)

    def count_ge(t):
        return jnp.sum((key >= t).astype(F32), axis=1, keepdims=True)

    int_min = jnp.int32(-2 ** 31)
    thr = jnp.where(count_ge(jnp.int32(0)) >= kf, jnp.int32(0), int_min)
    thr = jnp.broadcast_to(thr, (key.shape[0], 1)).astype(jnp.int32)

    def body(i, thr):
        cand = thr + jnp.left_shift(jnp.int32(1), jnp.int32(30) - i)
        return jnp.where(count_ge(cand) >= kf, cand, thr)

    return lax.fori_loop(0, 31, body, thr)


def _sortable_key(scores):
    bits = lax.bitcast_convert_type(scores + 0.0, jnp.int32)
    return jnp.where(bits < 0, bits ^ jnp.int32(0x7FFFFFFF), bits)


def _rmsnorm_body(x_ref, g_ref, o_ref):
    x = x_ref[...]
    ms = jnp.mean(x * x, axis=-1, keepdims=True)
    o_ref[...] = ((x * lax.rsqrt(ms + EPS)) * g_ref[...]).astype(o_ref.dtype)


def _rmsnorm(x, g, out_dtype, bm):
    m, d = x.shape
    bm = min(bm, m)
    return pl.pallas_call(
        _rmsnorm_body,
        grid=(m // bm,),
        in_specs=[pl.BlockSpec((bm, d), lambda i: (i, 0)), pl.BlockSpec((1, d), lambda i: (0, 0))],
        out_specs=pl.BlockSpec((bm, d), lambda i: (i, 0)),
        out_shape=jax.ShapeDtypeStruct((m, d), out_dtype),
        compiler_params=_params(1, 40),
        name="rmsnorm",
    )(x, g.reshape(1, d))


def _mm_body(x_ref, w_ref, o_ref):
    o_ref[...] = jnp.dot(x_ref[...], w_ref[...], preferred_element_type=F32)


def _mm_nt_body(x_ref, wt_ref, o_ref):
    o_ref[...] = _dot_nt(x_ref[...], wt_ref[...])


def _matmul_nt(x, wt, bm, bn, name):
    m, k = x.shape
    n = wt.shape[0]
    bm, bn = min(bm, m), min(bn, n)
    return pl.pallas_call(
        _mm_nt_body,
        grid=(m // bm, n // bn),
        in_specs=[pl.BlockSpec((bm, k), lambda i, j: (i, 0)), pl.BlockSpec((bn, k), lambda i, j: (j, 0))],
        out_specs=pl.BlockSpec((bm, bn), lambda i, j: (i, j)),
        out_shape=jax.ShapeDtypeStruct((m, n), F32),
        compiler_params=_params(2, 48),
        name=name,
    )(x, wt)


def _matmul(x, w, bm, bn, name):
    m, k = x.shape
    n = w.shape[1]
    bm, bn = min(bm, m), min(bn, n)
    return pl.pallas_call(
        _mm_body,
        grid=(m // bm, n // bn),
        in_specs=[pl.BlockSpec((bm, k), lambda i, j: (i, 0)), pl.BlockSpec((k, bn), lambda i, j: (0, j))],
        out_specs=pl.BlockSpec((bm, bn), lambda i, j: (i, j)),
        out_shape=jax.ShapeDtypeStruct((m, n), F32),
        compiler_params=_params(2, 48),
        name=name,
    )(x, w)


def _wprep_body(shifts, tab_ref, a_ref, b_ref, o_ref):
    shift = tab_ref[1, pl.program_id(1)]

    @pl.when(shift == 0)
    def _():
        o_ref[...] = a_ref[...].astype(o_ref.dtype)

    for sh in shifts:
        @pl.when(shift == sh)
        def _(sh=sh):
            x = jnp.concatenate([a_ref[...], b_ref[...]], axis=0)
            o_ref[...] = pltpu.roll(x, 2 * COL_TILE - sh, 0)[:COL_TILE, :].astype(o_ref.dtype)


def _wprep(wt, tiles, bk):
    n_src, k = wt.shape
    last = pl.cdiv(n_src, COL_TILE) - 1
    tab = jnp.asarray([[c // COL_TILE for c in tiles], [c % COL_TILE for c in tiles]], jnp.int32)
    shifts = tuple(sorted({c % COL_TILE for c in tiles} - {0}))
    bk = min(bk, k)
    grid_spec = pltpu.PrefetchScalarGridSpec(
        num_scalar_prefetch=1,
        grid=(k // bk, len(tiles)),
        in_specs=[pl.BlockSpec((COL_TILE, bk), lambda i, t, tab: (tab[0, t], i)),
                  pl.BlockSpec((COL_TILE, bk), lambda i, t, tab: (jnp.minimum(tab[0, t] + 1, last), i))],
        out_specs=pl.BlockSpec((COL_TILE, bk), lambda i, t, tab: (t, i)),
    )
    return pl.pallas_call(
        functools.partial(_wprep_body, shifts),
        grid_spec=grid_spec,
        out_shape=jax.ShapeDtypeStruct((COL_TILE * len(tiles), k), BF16),
        compiler_params=_params(2, 40),
        name="w_prep",
    )(tab, wt, wt)


def _cast_body(x_ref, o_ref):
    o_ref[...] = x_ref[...].astype(o_ref.dtype)


def _cast_bf16(w, bm):
    m, n = w.shape
    return pl.pallas_call(
        _cast_body,
        grid=(m // bm,),
        in_specs=[pl.BlockSpec((bm, n), lambda i: (i, 0))],
        out_specs=pl.BlockSpec((bm, n), lambda i: (i, 0)),
        out_shape=jax.ShapeDtypeStruct((m, n), BF16),
        compiler_params=_params(1, 40),
        name="w_cast",
    )(w)


def _post_body(ka_ref, s1_ref, s2_ref, tab_ref, bf_ref, ka_o, sm_o):
    c, sa, sb = tab_ref[0], tab_ref[1], tab_ref[2]
    for h in range(KVH_A):
        sl = slice(h * HEAD_DIM, (h + 1) * HEAD_DIM)
        ka_o[:, sl] = _rot(ka_ref[:, sl], c, sa, sb, HEAD_DIM // ROT_FRAC // 2)
    r = _rot(s1_ref[...], tab_ref[6], tab_ref[7], tab_ref[8], D_IDX // ROT_FRAC // 2)
    lf = _log_sigmoid(pltpu.roll(s2_ref[...], LF_LANE, 1) + bf_ref[...])
    lane = lax.broadcasted_iota(jnp.int32, r.shape, 1)
    sm_o[...] = jnp.where(lane < LF_LANE, r, jnp.where(lane < LF_LANE + H_B, lf, 0.0))


def _post(proj, tab, bf_row, seq_rows, bm):
    off = _layout()[0]
    m = proj.shape[0]
    bm = min(bm, m)
    tb = seq_rows // bm
    return pl.pallas_call(
        _post_body,
        grid=(m // bm,),
        in_specs=[pl.BlockSpec((bm, W_KV), lambda i: (i, off["ka"] // W_KV)),
                  pl.BlockSpec((bm, LANES), lambda i: (i, off["s1"] // LANES)),
                  pl.BlockSpec((bm, LANES), lambda i: (i, off["s2"] // LANES)),
                  pl.BlockSpec((9, bm, LANES), lambda i: (0, i % tb, 0)),
                  pl.BlockSpec((1, LANES), lambda i: (0, 0))],
        out_specs=[pl.BlockSpec((bm, W_KV), lambda i: (i, 0)), pl.BlockSpec((bm, LANES), lambda i: (i, 0))],
        out_shape=[jax.ShapeDtypeStruct((m, W_KV), F32), jax.ShapeDtypeStruct((m, LANES), F32)],
        compiler_params=_params(1, 40),
        name="kv_post",
    )(proj, proj, proj, tab, bf_row)


def _qrot_body(qa_ref, iq_ref, tab_ref, qa_o, iq_o):
    for h in range(H_A):
        sl = slice(h * HEAD_DIM, (h + 1) * HEAD_DIM)
        qa_o[:, sl] = _rot(qa_ref[:, sl], tab_ref[0], tab_ref[1], tab_ref[2], HEAD_DIM // ROT_FRAC // 2)
    for hp in range(H_IDX * D_IDX // LANES):
        sl = slice(hp * LANES, (hp + 1) * LANES)
        iq_o[:, sl] = _rot(iq_ref[:, sl], tab_ref[3], tab_ref[4], tab_ref[5], D_IDX // ROT_FRAC // 2)


def _qrot(proj, tab):
    off = _layout()[0]
    m = proj.shape[0]
    wi = H_IDX * D_IDX
    return pl.pallas_call(
        _qrot_body,
        grid=(1,),
        in_specs=[pl.BlockSpec((m, W_A), lambda i: (0, off["qa"] // W_A)),
                  pl.BlockSpec((m, wi), lambda i: (0, off["iq"] // wi)),
                  pl.BlockSpec((9, m, LANES), lambda i: (0, 0, 0))],
        out_specs=[pl.BlockSpec((m, W_A), lambda i: (0, 0)), pl.BlockSpec((m, wi), lambda i: (0, 0))],
        out_shape=[jax.ShapeDtypeStruct((m, W_A), F32), jax.ShapeDtypeStruct((m, wi), F32)],
        compiler_params=_params(1, 32),
        name="q_rotary",
    )(proj, proj, tab)


def _cumsum_body(sm_ref, ccol_ref, crow_ref):
    r = lax.broadcasted_iota(jnp.int32, (LANES, LANES), 0)
    c = lax.broadcasted_iota(jnp.int32, (LANES, LANES), 1)
    tri = (c <= r).astype(BF16)
    keep = (c >= LF_LANE) & (c < LF_LANE + H_B)
    carry = jnp.zeros((1, LANES), F32)
    for blk in range(ccol_ref.shape[0] // LANES):
        rows = slice(blk * LANES, (blk + 1) * LANES)
        x = jnp.where(keep, sm_ref[rows, :], 0.0)
        hi, mid, lo = _split3(x)
        cb = (jnp.dot(tri, hi, preferred_element_type=F32) + jnp.dot(tri, mid, preferred_element_type=F32)
              + jnp.dot(tri, lo, preferred_element_type=F32)) + carry
        carry = cb[LANES - 1:LANES, :]
        ccol_ref[rows, :] = cb
        per = crow_ref.shape[3] // LANES
        crow_ref[0, blk // per, :, (blk % per) * LANES:(blk % per + 1) * LANES] = cb.T


def _cumsum(small, batch, seq, tk):
    return pl.pallas_call(
        _cumsum_body,
        grid=(batch,),
        in_specs=[pl.BlockSpec((seq, LANES), lambda b: (b, 0))],
        out_specs=[pl.BlockSpec((seq, LANES), lambda b: (b, 0)),
                   pl.BlockSpec((1, seq // tk, LANES, tk), lambda b: (b, 0, 0, 0))],
        out_shape=[jax.ShapeDtypeStruct((batch * seq, LANES), F32),
                   jax.ShapeDtypeStruct((batch, seq // tk, LANES, tk), F32)],
        compiler_params=_params(1, 32),
        name="fox_cumsum",
    )(small)


def _softmax_av(s, v):
    m = jnp.max(s, axis=1, keepdims=True)
    p = jnp.exp(s - m)
    l = jnp.sum(p, axis=1, keepdims=True)
    return jnp.dot(p.astype(BF16), v, preferred_element_type=F32) / l


def _lane_pieces(x):
    return [x[:, t * LANES:(t + 1) * LANES] for t in range(x.shape[1] // LANES)]


def _lane_tile(x, width):
    return jnp.concatenate([x] * (width // LANES), axis=1)


def _attend_two_pass(n_heads, hd, qi, logits, values, last_mask, z_ref, u_ref, mx_ref, ls_ref, acc_ref):
    tq = tk = mx_ref.shape[1]

    def each_chunk(step):
        if last_mask is None:
            lax.fori_loop(0, qi + 1, lambda ci, c: (step(ci, None), c)[1], 0)
        else:
            lax.fori_loop(0, qi, lambda ci, c: (step(ci, None), c)[1], 0)
            step(qi, last_mask)

    def masked(ci, h, mask):
        s = logits(ci, h)
        return s if mask is None else s + mask

    mx_ref[...] = jnp.full(mx_ref.shape, NEG, F32)

    def row_max(ci, mask):
        for h in range(n_heads):
            m = mx_ref[h]
            for piece in _lane_pieces(masked(ci, h, mask)):
                m = jnp.maximum(m, piece)
            mx_ref[h] = m

    each_chunk(row_max)
    for h in range(n_heads):
        mx_ref[h] = jnp.broadcast_to(jnp.max(mx_ref[h], axis=1, keepdims=True), (tq, LANES))
    ls_ref[...] = jnp.zeros(ls_ref.shape, F32)
    acc_ref[...] = jnp.zeros(acc_ref.shape, F32)

    def accumulate(ci, mask):
        for h in range(n_heads):
            sl = slice(h * hd, (h + 1) * hd)
            p = jnp.exp(masked(ci, h, mask) - _lane_tile(mx_ref[h], tk))
            l = ls_ref[h]
            for piece in _lane_pieces(p):
                l = l + piece
            ls_ref[h] = l
            acc_ref[:, sl] += jnp.dot(p.astype(BF16), values(ci, h), preferred_element_type=F32)

    each_chunk(accumulate)
    for h in range(n_heads):
        sl = slice(h * hd, (h + 1) * hd)
        l = jnp.sum(ls_ref[h], axis=1, keepdims=True)
        u_ref[:, sl] = ((acc_ref[:, sl] / l) * _silu(z_ref[:, sl])).astype(u_ref.dtype)


def _chunk_rows(c, tk):
    return pl.ds(pl.multiple_of(c * tk, tk), tk)


def _dsa_prompt_body(topk, q_ref, iq_ref, sm_ref, z_ref, k_ref, v_ref, ik_ref, tab_ref, u_ref,
                     xb_ref, w_ref, key_ref, bias_ref, qb_ref, mx_ref, ls_ref, acc_ref):
    tq = q_ref.shape[0]
    qi = pl.program_id(1)
    r = lax.broadcasted_iota(jnp.int32, (tq, tq), 0)
    c = lax.broadcasted_iota(jnp.int32, (tq, tq), 1)
    tri = c <= r

    for hp in range(H_IDX * D_IDX // LANES):
        sl = slice(hp * LANES, (hp + 1) * LANES)
        xb_ref[:, sl] = _rot(iq_ref[:, sl], tab_ref[3], tab_ref[4], tab_ref[5], D_IDX // ROT_FRAC // 2).astype(BF16)
    for h in range(H_IDX):
        w_ref[h] = jnp.broadcast_to(sm_ref[:, D_IDX + h:D_IDX + h + 1] * (H_IDX ** -0.5), (tq, LANES))

    def scores(ci):
        rows = _chunk_rows(ci, tq)
        acc = jnp.zeros((tq, tq), F32)
        for hp in range(H_IDX * D_IDX // LANES):
            xb = xb_ref[:, hp * LANES:(hp + 1) * LANES]
            for j in range(LANES // D_IDX):
                h = hp * (LANES // D_IDX) + j
                s = _dot_nt(xb, ik_ref[rows, j * LANES:(j + 1) * LANES]) * (D_IDX ** -0.5)
                acc = acc + jnp.maximum(s, 0.0) * _lane_tile(w_ref[h], tq)
        return acc

    def fill_keys(ci, carry):
        key_ref[ci] = _sortable_key(scores(ci))
        return carry

    lax.fori_loop(0, qi, fill_keys, 0)
    key_ref[qi] = _sortable_key(jnp.where(tri, scores(qi), -jnp.inf))

    kf = jnp.float32(topk)

    def count_ge(cand):
        def add(ci, cnt):
            hit = jnp.where(key_ref[ci] >= cand, 1.0, 0.0)
            for piece in _lane_pieces(hit):
                cnt = cnt + piece
            return cnt
        cnt = lax.fori_loop(0, qi + 1, add, jnp.zeros((tq, LANES), F32))
        return jnp.sum(cnt, axis=1, keepdims=True)

    zero = jnp.zeros((tq, 1), jnp.int32)
    thr = jnp.where(count_ge(zero) >= kf, zero, jnp.int32(-2 ** 31))

    def refine(i, thr):
        cand = thr + jnp.left_shift(jnp.int32(1), jnp.int32(30) - i)
        return jnp.where(count_ge(cand) >= kf, cand, thr)

    thr = lax.fori_loop(0, 31, refine, thr)

    need = kf - count_ge(thr + 1)
    prefix = (r <= c).astype(BF16)

    def keep_bias(ci, run, mask):
        kk = key_ref[ci]
        tie = kk == thr
        tie_f = jnp.where(tie, 1.0, 0.0)
        seen = jnp.dot(tie_f.astype(BF16), prefix, preferred_element_type=F32) + run
        keep = (kk > thr) | (tie & (seen <= need))
        if mask is not None:
            keep = keep & mask
        bias_ref[ci] = jnp.where(keep, 0.0, NEG)
        return run + jnp.sum(tie_f, axis=1, keepdims=True)

    run = lax.fori_loop(0, qi, lambda ci, run: keep_bias(ci, run, None), jnp.zeros((tq, 1), F32))
    keep_bias(qi, run, tri)

    grp = H_A // KVH_A
    for h in range(H_A):
        sl = slice(h * HEAD_DIM, (h + 1) * HEAD_DIM)
        qb_ref[:, sl] = _rot(q_ref[:, sl], tab_ref[0], tab_ref[1], tab_ref[2],
                             HEAD_DIM // ROT_FRAC // 2).astype(BF16)

    def kv_cols(h):
        return slice(h // grp * HEAD_DIM, (h // grp + 1) * HEAD_DIM)

    def logits(ci, h):
        qh = qb_ref[:, h * HEAD_DIM:(h + 1) * HEAD_DIM]
        return _dot_nt(qh, k_ref[_chunk_rows(ci, tq), kv_cols(h)]) * (HEAD_DIM ** -0.5) + bias_ref[ci]

    def values(ci, h):
        return v_ref[_chunk_rows(ci, tq), kv_cols(h)]

    _attend_two_pass(H_A, HEAD_DIM, qi, logits, values, None, z_ref, u_ref, mx_ref, ls_ref, acc_ref)


def _dsa_prompt(proj, small, ka_bf, va_bf, ik_bf, tab, batch, seq, tq):
    off = _layout()[0]
    nq = seq // tq
    wi = H_IDX * D_IDX
    topk = min(TOPK_MAX, seq // 4)
    row = lambda b, q: b * nq + q
    return pl.pallas_call(
        functools.partial(_dsa_prompt_body, topk),
        grid=(batch, nq),
        scratch_shapes=[pltpu.VMEM((tq, wi), BF16), pltpu.VMEM((H_IDX, tq, LANES), F32),
                        pltpu.VMEM((nq, tq, tq), jnp.int32), pltpu.VMEM((nq, tq, tq), F32),
                        pltpu.VMEM((tq, W_A), BF16), pltpu.VMEM((H_A, tq, LANES), F32),
                        pltpu.VMEM((H_A, tq, LANES), F32), pltpu.VMEM((tq, W_A), F32)],
        in_specs=[pl.BlockSpec((tq, W_A), lambda b, q: (row(b, q), off["qa"] // W_A)),
                  pl.BlockSpec((tq, wi), lambda b, q: (row(b, q), off["iq"] // wi)),
                  pl.BlockSpec((tq, LANES), lambda b, q: (row(b, q), 0)),
                  pl.BlockSpec((tq, W_A), lambda b, q: (row(b, q), off["za"] // W_A)),
                  pl.BlockSpec((seq, W_KV), lambda b, q: (b, 0)),
                  pl.BlockSpec((seq, W_KV), lambda b, q: (b, 0)),
                  pl.BlockSpec((seq, 2 * LANES), lambda b, q: (b, 0)),
                  pl.BlockSpec((9, tq, LANES), lambda b, q: (0, q, 0))],
        out_specs=pl.BlockSpec((tq, W_A), lambda b, q: (row(b, q), 0)),
        out_shape=jax.ShapeDtypeStruct((batch * seq, W_A), BF16),
        compiler_params=_params(2, 56),
        name="dsa_prompt",
    )(proj, proj, small, proj, ka_bf, va_bf, ik_bf, tab)


def _fox_prompt_body(q_ref, z_ref, k_ref, v_ref, ccol_ref, crow_ref, u_ref, qb_ref, cq_ref, mx_ref, ls_ref, acc_ref):
    tq = q_ref.shape[0]
    qi = pl.program_id(1)
    r = lax.broadcasted_iota(jnp.int32, (tq, tq), 0)
    c = lax.broadcasted_iota(jnp.int32, (tq, tq), 1)
    diag_bias = jnp.where(c <= r, 0.0, NEG)
    grp = H_B // KVH_B
    qb_ref[...] = q_ref[...].astype(BF16)
    for h in range(H_B):
        cq_ref[h] = jnp.broadcast_to(ccol_ref[:, LF_LANE + h:LF_LANE + h + 1], (tq, LANES))

    def kv_cols(h):
        return slice(h // grp * HEAD_DIM, (h // grp + 1) * HEAD_DIM)

    def logits(ci, h):
        qh = qb_ref[:, h * HEAD_DIM:(h + 1) * HEAD_DIM]
        s = _dot_nt(qh, k_ref[_chunk_rows(ci, tq), kv_cols(h)]) * (HEAD_DIM ** -0.5)
        return s + _lane_tile(cq_ref[h], tq) - crow_ref[0, ci, LF_LANE + h:LF_LANE + h + 1, :]

    def values(ci, h):
        return v_ref[_chunk_rows(ci, tq), kv_cols(h)]

    _attend_two_pass(H_B, HEAD_DIM, qi, logits, values, diag_bias, z_ref, u_ref, mx_ref, ls_ref, acc_ref)


def _fox_prompt(proj, kb_bf, vb_bf, ccol, crow, batch, seq, tq):
    off = _layout()[0]
    nq = seq // tq
    row = lambda b, q: b * nq + q
    return pl.pallas_call(
        _fox_prompt_body,
        grid=(batch, nq),
        in_specs=[pl.BlockSpec((tq, W_B), lambda b, q: (row(b, q), off["qb"] // W_B)),
                  pl.BlockSpec((tq, W_B), lambda b, q: (row(b, q), off["zb"] // W_B)),
                  pl.BlockSpec((seq, W_KV), lambda b, q: (b, 0)),
                  pl.BlockSpec((seq, W_KV), lambda b, q: (b, 0)),
                  pl.BlockSpec((tq, LANES), lambda b, q: (row(b, q), 0)),
                  pl.BlockSpec((1, nq, LANES, tq), lambda b, q: (b, 0, 0, 0))],
        out_specs=pl.BlockSpec((tq, W_B), lambda b, q: (row(b, q), 0)),
        scratch_shapes=[pltpu.VMEM((tq, W_B), BF16), pltpu.VMEM((H_B, tq, LANES), F32),
                        pltpu.VMEM((H_B, tq, LANES), F32), pltpu.VMEM((H_B, tq, LANES), F32),
                        pltpu.VMEM((tq, W_B), F32)],
        out_shape=jax.ShapeDtypeStruct((batch * seq, W_B), BF16),
        compiler_params=_params(2, 56),
        name="fox_prompt",
    )(proj, proj, kb_bf, vb_bf, ccol, crow)


def _cross_prompt_body(q_ref, z_ref, k_ref, v_ref, u_ref):
    for h in range(H_C):
        sl = slice(h * HD_C, (h + 1) * HD_C)
        s = _dot_nt(q_ref[:, sl].astype(BF16), k_ref[:, sl]) * (HD_C ** -0.5)
        o = _softmax_av(s, v_ref[:, sl])
        u_ref[:, sl] = (o * _silu(z_ref[:, sl])).astype(u_ref.dtype)


def _cross_prompt(proj, mk_bf, mv_bf, batch, seq, tq):
    off = _layout()[0]
    nq = seq // tq
    row = lambda b, q: b * nq + q
    return pl.pallas_call(
        _cross_prompt_body,
        grid=(batch, nq),
        in_specs=[pl.BlockSpec((tq, W_C), lambda b, q: (row(b, q), off["qc"] // W_C)),
                  pl.BlockSpec((tq, W_C), lambda b, q: (row(b, q), off["zc"] // W_C)),
                  pl.BlockSpec((N_MEM, W_C), lambda b, q: (b, 0)),
                  pl.BlockSpec((N_MEM, W_C), lambda b, q: (b, 0))],
        out_specs=pl.BlockSpec((tq, W_C), lambda b, q: (row(b, q), 0)),
        out_shape=jax.ShapeDtypeStruct((batch * seq, W_C), BF16),
        compiler_params=_params(2, 40),
        name="cross_prompt",
    )(proj, proj, mk_bf, mv_bf)


def _merge_body(ua_ref, ub_ref, uc_ref, wa_ref, wb_ref, wc_ref, ga_ref, gb_ref, gc_ref, h_ref):
    h = (_sigmoid(ga_ref[...]) * jnp.dot(ua_ref[...], wa_ref[...], preferred_element_type=F32)
         + _sigmoid(gb_ref[...]) * jnp.dot(ub_ref[...], wb_ref[...], preferred_element_type=F32)
         + _sigmoid(gc_ref[...]) * jnp.dot(uc_ref[...], wc_ref[...], preferred_element_type=F32))
    h_ref[...] = h.astype(h_ref.dtype)


def _merge(ua, ub, uc, wa, wb, wc, proj, bm, bn):
    off = _layout()[0]
    m = ua.shape[0]
    d = wa.shape[1]
    bm, bn = min(bm, m), min(bn, d)
    g0 = off["gates"] // bn
    nd = d // bn
    return pl.pallas_call(
        _merge_body,
        grid=(m // bm, nd),
        in_specs=[pl.BlockSpec((bm, W_A), lambda i, j: (i, 0)),
                  pl.BlockSpec((bm, W_B), lambda i, j: (i, 0)),
                  pl.BlockSpec((bm, W_C), lambda i, j: (i, 0)),
                  pl.BlockSpec((W_A, bn), lambda i, j: (0, j)),
                  pl.BlockSpec((W_B, bn), lambda i, j: (0, j)),
                  pl.BlockSpec((W_C, bn), lambda i, j: (0, j)),
                  pl.BlockSpec((bm, bn), lambda i, j: (i, g0 + j)),
                  pl.BlockSpec((bm, bn), lambda i, j: (i, g0 + nd + j)),
                  pl.BlockSpec((bm, bn), lambda i, j: (i, g0 + 2 * nd + j))],
        out_specs=pl.BlockSpec((bm, bn), lambda i, j: (i, j)),
        out_shape=jax.ShapeDtypeStruct((m, d), BF16),
        compiler_params=_params(2, 48),
        name="merge",
    )(ua, ub, uc, wa, wb, wc, proj, proj, proj)


def _outproj_body(x_ref, h_ref, w_ref, y_ref):
    y_ref[...] = x_ref[...] + jnp.dot(h_ref[...], w_ref[...], preferred_element_type=F32)


def _outproj(x, h, w, bm, bn):
    m, d = x.shape
    bm, bn = min(bm, m), min(bn, d)
    return pl.pallas_call(
        _outproj_body,
        grid=(m // bm, d // bn),
        in_specs=[pl.BlockSpec((bm, bn), lambda i, j: (i, j)),
                  pl.BlockSpec((bm, d), lambda i, j: (i, 0)),
                  pl.BlockSpec((d, bn), lambda i, j: (0, j))],
        out_specs=pl.BlockSpec((bm, bn), lambda i, j: (i, j)),
        out_shape=jax.ShapeDtypeStruct((m, d), F32),
        compiler_params=_params(2, 48),
        name="out_proj",
    )(x, h, w)


def _idx_sample_body(pps, pt_ref, iq_ref, w_ref, ikn_ref, *rest):
    pages = rest[:pps]
    sc_ref, own_ref = rest[pps], rest[pps + 1]
    iq = iq_ref[0]
    w = w_ref[0] * (H_IDX ** -0.5)
    for i in range(pps):
        s = jnp.dot(iq, pages[i][...].astype(BF16), preferred_element_type=F32) * (D_IDX ** -0.5)
        sc_ref[0, :, i * PAGE_SIZE:(i + 1) * PAGE_SIZE] = jnp.sum(jnp.maximum(s, 0.0) * w, axis=0, keepdims=True)

    @pl.when(pl.program_id(1) == 0)
    def _():
        kn = ikn_ref[0].astype(F32)
        s = jnp.sum(iq.astype(F32) * kn, axis=1, keepdims=True) * (D_IDX ** -0.5)
        own = jnp.sum(jnp.maximum(s, 0.0) * w, axis=0, keepdims=True)
        lane = lax.broadcasted_iota(jnp.int32, (1, LANES), 1)
        own_ref[0] = jnp.where(lane == 0, own, -jnp.inf)


def _idx_sample(page_table, iq_bf, iw, ik_new_bf, pool_idx, pps):
    b, n_pages = page_table.shape
    npc = n_pages // pps
    page_specs = [pl.BlockSpec((None, None, D_IDX, PAGE_SIZE),
                               lambda bi, pc, pt, i=i: (0, pt[bi, pc * pps + i], 0, 0)) for i in range(pps)]
    grid_spec = pltpu.PrefetchScalarGridSpec(
        num_scalar_prefetch=1,
        grid=(b, npc),
        in_specs=[pl.BlockSpec((1, H_IDX, D_IDX), lambda bi, pc, pt: (bi, 0, 0)),
                  pl.BlockSpec((1, H_IDX, 1), lambda bi, pc, pt: (bi, 0, 0)),
                  pl.BlockSpec((1, 1, D_IDX), lambda bi, pc, pt: (bi, 0, 0))] + page_specs,
        out_specs=[pl.BlockSpec((1, 1, pps * PAGE_SIZE), lambda bi, pc, pt: (bi, 0, pc)),
                   pl.BlockSpec((1, 1, LANES), lambda bi, pc, pt: (bi, 0, 0))],
    )
    return pl.pallas_call(
        functools.partial(_idx_sample_body, pps),
        grid_spec=grid_spec,
        out_shape=[jax.ShapeDtypeStruct((b, 1, n_pages * PAGE_SIZE), F32),
                   jax.ShapeDtypeStruct((b, 1, LANES), F32)],
        compiler_params=_params(2, 32),
        name="idx_sample",
    )(page_table, iq_bf, iw, ik_new_bf, *([pool_idx] * pps))


def _topk_sample_body(topk, n_valid, grp, sc_ref, bias_ref):
    sc = sc_ref[...]
    lane = lax.broadcasted_iota(jnp.int32, sc.shape, 1)
    valid = lane < n_valid
    key = _sortable_key(jnp.where(valid, sc, -jnp.inf))
    thr = _kth_largest_key(key, topk)
    need = jnp.float32(topk) - jnp.sum(jnp.where(key > thr, 1.0, 0.0), axis=1, keepdims=True)
    pr = lax.broadcasted_iota(jnp.int32, (LANES, LANES), 0)
    pc = lax.broadcasted_iota(jnp.int32, (LANES, LANES), 1)
    prefix = (pr <= pc).astype(BF16)
    k = lax.broadcasted_iota(jnp.int32, (LANES, LANES * grp), 0)
    r = lax.broadcasted_iota(jnp.int32, (LANES, LANES * grp), 1)
    rep = ((r >= k * grp) & (r < (k + 1) * grp)).astype(BF16)
    run = jnp.zeros((sc.shape[0], 1), F32)
    for c in range(sc.shape[1] // LANES):
        sl = slice(c * LANES, (c + 1) * LANES)
        kk = key[:, sl]
        tie = kk == thr
        tie_f = jnp.where(tie, 1.0, 0.0)
        seen = jnp.dot(tie_f.astype(BF16), prefix, preferred_element_type=F32) + run
        in_range = lax.broadcasted_iota(jnp.int32, kk.shape, 1) < n_valid - c * LANES
        keep = ((kk > thr) | (tie & (seen <= need))) & in_range
        flags = jnp.dot(jnp.where(keep, 1.0, 0.0).astype(BF16), rep, preferred_element_type=F32)
        bias_ref[:, c * LANES * grp:(c + 1) * LANES * grp] = jnp.where(flags > 0.5, 0.0, NEG)
        run = run + jnp.sum(tie_f, axis=1, keepdims=True)


def _topk_sample(scores, n_valid, grp):
    topk = min(TOPK_MAX, n_valid // 4)
    b, n = scores.shape
    return pl.pallas_call(
        functools.partial(_topk_sample_body, topk, n_valid, grp),
        grid=(1,),
        in_specs=[pl.BlockSpec((b, n), lambda i: (0, 0))],
        out_specs=pl.BlockSpec((b, n * grp), lambda i: (0, 0)),
        out_shape=jax.ShapeDtypeStruct((b, n * grp), F32),
        compiler_params=_params(1, 40),
        name="topk_sample",
    )(scores)


def _fox_bias_body(pps, grp, pt_ref, lfn_ref, *rest):
    pages = rest[:pps]
    d_ref, x_ref, carry_ref = rest[pps], rest[pps + 1], rest[pps + 2]
    rows = PAGE_SIZE * grp
    j = lax.broadcasted_iota(jnp.int32, (LANES, rows), 0)
    r = lax.broadcasted_iota(jnp.int32, (LANES, rows), 1)
    upper = (r < j * grp).astype(BF16)

    @pl.when(pl.program_id(1) == 0)
    def _():
        x_ref[...] = jnp.zeros_like(x_ref)
        x_ref[0, 0:1, :] = lfn_ref[0]
        carry_ref[...] = x_ref[0].T[0:HEAD_ROWS, 0:1]
        x_ref[0, 0:1, :] = jnp.zeros((1, LANES), F32)

    carry = carry_ref[...]
    for i in reversed(range(pps)):
        x_ref[i, :, 0:H_B] = pages[i][...]
        xt = x_ref[i].T[0:HEAD_ROWS, :]
        hi, mid, lo = _split3(xt)
        d_loc = (jnp.dot(hi, upper, preferred_element_type=F32) + jnp.dot(mid, upper, preferred_element_type=F32)
                 + jnp.dot(lo, upper, preferred_element_type=F32))
        d_ref[0, :, i * rows:(i + 1) * rows] = d_loc + carry
        carry = carry + jnp.sum(xt, axis=1, keepdims=True)
    carry_ref[...] = carry


def _fox_bias(page_table, lf_new, pool_lf, pps, grp):
    b, n_pages = page_table.shape
    npc = n_pages // pps
    rows = PAGE_SIZE * grp
    page_specs = [pl.BlockSpec((None, None, PAGE_SIZE, H_B),
                               lambda bi, pc, pt, i=i: (0, pt[bi, (npc - 1 - pc) * pps + i], 0, 0))
                  for i in range(pps)]
    grid_spec = pltpu.PrefetchScalarGridSpec(
        num_scalar_prefetch=1,
        grid=(b, npc),
        in_specs=[pl.BlockSpec((1, 1, LANES), lambda bi, pc, pt: (bi, 0, 0))] + page_specs,
        out_specs=pl.BlockSpec((1, HEAD_ROWS, pps * rows), lambda bi, pc, pt: (bi, 0, npc - 1 - pc)),
        scratch_shapes=[pltpu.VMEM((pps, LANES, LANES), F32), pltpu.VMEM((HEAD_ROWS, 1), F32)],
    )
    return pl.pallas_call(
        functools.partial(_fox_bias_body, pps, grp),
        grid_spec=grid_spec,
        out_shape=jax.ShapeDtypeStruct((b, HEAD_ROWS, n_pages * rows), F32),
        compiler_params=_params(2, 32),
        name="fox_bias_sample",
    )(page_table, lf_new, *([pool_lf] * pps))


def _paged_body(pps, scale, pt_ref, q_ref, bias_ref, gm_ref, kn_ref, vn_ref, nb_ref, z_ref, *rest):
    kps, vps = rest[:pps], rest[pps:2 * pps]
    u_ref, m_ref, l_ref, acc_ref = rest[2 * pps:2 * pps + 4]
    pc = pl.program_id(1)
    rows = kps[0].shape[0]
    q = q_ref[0]

    @pl.when(pc == 0)
    def _():
        kn = kn_ref[0].astype(BF16).astype(F32)
        vn = vn_ref[0].astype(BF16).astype(F32)
        nb = nb_ref[0]
        on = nb > 0.5 * NEG
        s0 = jnp.sum(q.astype(F32) * kn, axis=1, keepdims=True) * scale + nb
        m_ref[...] = jnp.where(on, s0, NEG)
        l_ref[...] = jnp.where(on, 1.0, 0.0)
        acc_ref[...] = jnp.where(on, vn, 0.0)

    ss = [(_dot_nt(q, kps[i][...].astype(BF16)) * scale + bias_ref[0, :, i * rows:(i + 1) * rows]) + gm_ref[...]
          for i in range(pps)]
    m_old = m_ref[...]
    m_new = m_old
    for s in ss:
        m_new = jnp.maximum(m_new, jnp.max(s, axis=1, keepdims=True))
    alpha = jnp.exp(m_old - m_new)
    l_new = alpha * l_ref[...]
    acc = alpha * acc_ref[...]
    for i, s in enumerate(ss):
        pf = jnp.where(s > 0.5 * NEG, jnp.exp(s - m_new), 0.0)
        l_new = l_new + jnp.sum(pf, axis=1, keepdims=True)
        acc = acc + jnp.dot(pf.astype(BF16), vps[i][...].astype(BF16), preferred_element_type=F32)
    l_ref[...] = l_new
    acc_ref[...] = acc
    m_ref[...] = m_new

    @pl.when(pc == pl.num_programs(1) - 1)
    def _():
        u_ref[0] = (acc_ref[...] / l_ref[...]) * _silu(z_ref[0])


def _paged_attn(page_idx, q, bias, gmask, k_new, v_new, new_bias, z, pool_k, pool_v, pps, scale, name, paged=True):
    b, n_pages = page_idx.shape
    rows, hd = (pool_k.shape[1] if paged else pool_k.shape[1] // n_pages), pool_k.shape[2]
    rb = bias.shape[1]
    npc = n_pages // pps
    if paged:
        kspecs = [pl.BlockSpec((None, rows, hd), lambda bi, pc, pt, i=i: (pt[bi, pc * pps + i], 0, 0))
                  for i in range(pps)]
    else:
        kspecs = [pl.BlockSpec((None, rows, hd), lambda bi, pc, pt, i=i: (bi, pc * pps + i, 0)) for i in range(pps)]
    per_b = lambda bi, pc, pt: (bi, 0, 0)
    head_blk = pl.BlockSpec((1, HEAD_ROWS, hd), per_b)
    grid_spec = pltpu.PrefetchScalarGridSpec(
        num_scalar_prefetch=1,
        grid=(b, npc),
        in_specs=[head_blk,
                  pl.BlockSpec((1, rb, pps * rows), lambda bi, pc, pt: (bi, 0, pc)),
                  pl.BlockSpec((HEAD_ROWS, rows), lambda bi, pc, pt: (0, 0)),
                  head_blk, head_blk,
                  pl.BlockSpec((1, HEAD_ROWS, 1), per_b),
                  head_blk] + kspecs + kspecs,
        out_specs=head_blk,
        scratch_shapes=[pltpu.VMEM((HEAD_ROWS, 1), F32), pltpu.VMEM((HEAD_ROWS, 1), F32),
                        pltpu.VMEM((HEAD_ROWS, hd), F32)],
    )
    return pl.pallas_call(
        functools.partial(_paged_body, pps, scale),
        grid_spec=grid_spec,
        out_shape=jax.ShapeDtypeStruct((b, HEAD_ROWS, hd), F32),
        compiler_params=_params(2, 48),
        name=name,
    )(page_idx, q, bias, gmask, k_new, v_new, new_bias, z, *([pool_k] * pps), *([pool_v] * pps))


def _rotary_tables(pos):
    def base(width):
        rot = width // ROT_FRAC
        half = rot // 2
        inv = jnp.power(jnp.float32(ROPE_THETA), -jnp.arange(half, dtype=F32) * (2.0 / rot))
        ang = pos.astype(F32)[:, None] * inv[None, :]
        cos, sin = jnp.cos(ang), jnp.sin(ang)
        n = pos.shape[0]
        ones, zeros = jnp.ones((n, width - rot), F32), jnp.zeros((n, width - rot), F32)
        zh = jnp.zeros((n, half), F32)
        return (jnp.concatenate([cos, cos, ones], 1), jnp.concatenate([-sin, zh, zeros], 1),
                jnp.concatenate([zh, sin, zeros], 1))

    c1, a1, b1 = base(HEAD_DIM)
    c2, a2, b2 = base(D_IDX)
    n = pos.shape[0]
    pad1, pad0 = jnp.ones((n, LANES - D_IDX), F32), jnp.zeros((n, LANES - D_IDX), F32)
    return jnp.stack([c1, a1, b1,
                      jnp.concatenate([c2, c2], 1), jnp.concatenate([a2, a2], 1), jnp.concatenate([b2, b2], 1),
                      jnp.concatenate([c2, pad1], 1), jnp.concatenate([a2, pad0], 1), jnp.concatenate([b2, pad0], 1)])


def _head_rows(x, n_heads, hd):
    b = x.shape[0]
    return jnp.pad(x.reshape(b, n_heads, hd), ((0, 0), (0, HEAD_ROWS - n_heads), (0, 0)))


def _group_rows(x, n_heads, n_groups, hd):
    b = x.shape[0]
    gid = np.minimum(np.arange(HEAD_ROWS), n_heads - 1) // (n_heads // n_groups)
    return x.reshape(b, n_groups, hd)[:, gid]


def _group_mask(n_heads, n_groups):
    gid = np.minimum(np.arange(HEAD_ROWS), n_heads - 1) // (n_heads // n_groups)
    r = np.arange(PAGE_SIZE * n_groups)
    return jnp.asarray(np.where((r[None, :] % n_groups) == gid[:, None], 0.0, NEG), F32)


def _trunk(x2d, proj, ua, ub, uc, w_a, w_b, w_c, w_o, g_final):
    h = _merge(ua, ub, uc, w_a, w_b, w_c, proj, 1024, 512)
    y = _outproj(x2d, h, w_o, 1024, 512)
    return _rmsnorm(y, g_final, F32, 256)


def kernel(x_prompt, x_sample, cache_a_k, cache_a_v, cache_a_idx, cache_b_k, cache_b_v, cache_b_logf, cache_mem_k,
           cache_mem_v, page_table, mem_prompt, g_norm, w_in, b_forget, w_br_a, w_br_b, w_br_c, w_out, g_mem,
           w_mem_kv, g_final):
    off, _, tiles = _layout()
    bsz, seq, d = x_prompt.shape
    dec = x_sample.shape[0]
    tq = min(256, seq)

    w_in_t = _wprep(jnp.transpose(w_in[0]), tiles, 2048)
    w_a, w_b, w_c, w_o, w_m = (_cast_bf16(w[0], 256) for w in (w_br_a, w_br_b, w_br_c, w_out, w_mem_kv))
    bf_row = jnp.zeros((1, LANES), F32).at[0, LF_LANE:LF_LANE + H_B].set(b_forget[0])

    xp = x_prompt.reshape(bsz * seq, d)
    proj = _matmul_nt(_rmsnorm(xp, g_norm[0], BF16, 256), w_in_t, 1024, 512, "in_proj")
    tab_p = _rotary_tables(jnp.arange(seq, dtype=jnp.int32))
    ka_rot, small = _post(proj, tab_p, bf_row, seq, 512)
    col = lambda name, width: proj[:, off[name]:off[name] + width]
    va, kb, vb = col("va", W_KV), col("kb", W_KV), col("vb", W_KV)
    ik = small[:, :D_IDX]
    logf = small[:, LF_LANE:LF_LANE + H_B]

    mem_n = _rmsnorm(mem_prompt.reshape(bsz * N_MEM, d), g_mem[0], BF16, 256)
    mkv = _matmul(mem_n, w_m, 1024, 1024, "mem_kv")
    mk, mv = mkv[:, :W_C], mkv[:, W_C:]

    ik_bf = ik.astype(BF16)
    ik_pad = jnp.zeros_like(ik_bf)
    ik2 = jnp.concatenate([ik_bf, ik_pad, ik_pad, ik_bf], axis=1)
    ua = _dsa_prompt(proj, small, ka_rot.astype(BF16), va.astype(BF16), ik2, tab_p, bsz, seq, tq)
    ccol, crow = _cumsum(small, bsz, seq, tq)
    ub = _fox_prompt(proj, kb.astype(BF16), vb.astype(BF16), ccol, crow, bsz, seq, tq)
    uc = _cross_prompt(proj, mk.astype(BF16), mv.astype(BF16), bsz, seq, tq)
    y_prompt = _trunk(xp, proj, ua, ub, uc, w_a, w_b, w_c, w_o, g_final).reshape(bsz, seq, d)

    xs = x_sample.reshape(dec, d)
    proj_s = _matmul_nt(_rmsnorm(xs, g_norm[0], BF16, dec), w_in_t, dec, 512, "in_proj_sample")
    tab_s = _rotary_tables(jnp.full((dec,), page_table.shape[1] * PAGE_SIZE, jnp.int32))
    ka_s, small_s = _post(proj_s, tab_s, bf_row, dec, dec)
    cols = lambda name, width: proj_s[:, off[name]:off[name] + width]
    va_s, kb_s, vb_s = cols("va", W_KV), cols("kb", W_KV), cols("vb", W_KV)
    ik_s = small_s[:, :D_IDX]
    iw_s = small_s[:, D_IDX:D_IDX + H_IDX]
    logf_s = small_s[:, LF_LANE:LF_LANE + H_B]
    qa_s, iq_s = _qrot(proj_s, tab_s)

    n_pages = page_table.shape[1]
    pps = min(16, n_pages)
    pps_small = min(32, n_pages)
    scores, own = _idx_sample(page_table, iq_s.astype(BF16).reshape(dec, H_IDX, D_IDX), iw_s.reshape(dec, H_IDX, 1),
                              ik_s.astype(BF16).reshape(dec, 1, D_IDX), jnp.transpose(cache_a_idx, (0, 1, 3, 2)),
                              pps_small)
    n_keys = n_pages * PAGE_SIZE
    n_pool = cache_a_k.shape[1]
    flat = lambda pool, g, hd: pool.reshape(n_pool, PAGE_SIZE * g, hd)
    sel_bias = _topk_sample(jnp.concatenate([scores[:, 0], own[:, 0]], axis=1), n_keys + 1, KVH_A)

    scale = HEAD_DIM ** -0.5
    ua_full = _paged_attn(
        page_table, _head_rows(qa_s, H_A, HEAD_DIM).astype(BF16), sel_bias[:, None, :n_keys * KVH_A],
        _group_mask(H_A, KVH_A), _group_rows(ka_s, H_A, KVH_A, HEAD_DIM), _group_rows(va_s, H_A, KVH_A, HEAD_DIM),
        jnp.broadcast_to(sel_bias[:, None, n_keys * KVH_A:n_keys * KVH_A + 1], (dec, HEAD_ROWS, 1)),
        _head_rows(cols("za", W_A), H_A, HEAD_DIM),
        flat(cache_a_k, KVH_A, HEAD_DIM), flat(cache_a_v, KVH_A, HEAD_DIM), pps, scale, "dsa_sample")

    lf_new = jnp.pad(logf_s, ((0, 0), (0, LANES - H_B))).reshape(dec, 1, LANES)
    d_bias = _fox_bias(page_table, lf_new, cache_b_logf, min(16, n_pages), KVH_B)
    ub_full = _paged_attn(
        page_table, _head_rows(cols("qb", W_B), H_B, HEAD_DIM).astype(BF16), d_bias,
        _group_mask(H_B, KVH_B), _group_rows(kb_s, H_B, KVH_B, HEAD_DIM), _group_rows(vb_s, H_B, KVH_B, HEAD_DIM),
        jnp.zeros((dec, HEAD_ROWS, 1), F32), _head_rows(cols("zb", W_B), H_B, HEAD_DIM),
        flat(cache_b_k, KVH_B, HEAD_DIM), flat(cache_b_v, KVH_B, HEAD_DIM), pps, scale, "fox_sample")

    mem_pages = N_MEM // PAGE_SIZE
    uc_full = _paged_attn(
        jnp.zeros((dec, mem_pages), jnp.int32), _head_rows(cols("qc", W_C), H_C, HD_C).astype(BF16),
        jnp.zeros((dec, 1, N_MEM * H_C), F32), _group_mask(H_C, H_C),
        jnp.zeros((dec, HEAD_ROWS, HD_C), F32), jnp.zeros((dec, HEAD_ROWS, HD_C), F32),
        jnp.full((dec, HEAD_ROWS, 1), NEG, F32), _head_rows(cols("zc", W_C), H_C, HD_C),
        cache_mem_k.reshape(dec, N_MEM * H_C, HD_C), cache_mem_v.reshape(dec, N_MEM * H_C, HD_C),
        mem_pages, HD_C ** -0.5, "cross_sample", paged=False)

    ua_s = ua_full[:, :H_A].reshape(dec, W_A).astype(BF16)
    ub_s = ub_full[:, :H_B].reshape(dec, W_B).astype(BF16)
    uc_s = uc_full[:, :H_C].reshape(dec, W_C).astype(BF16)
    y_sample = _trunk(xs, proj_s, ua_s, ub_s, uc_s, w_a, w_b, w_c, w_o, g_final).reshape(dec, 1, d)

    st = lambda a, b, t, *tail: a.reshape((1, b, t) + tail)
    return (y_prompt, y_sample,
            st(ka_rot, bsz, seq, KVH_A, HEAD_DIM), st(va, bsz, seq, KVH_A, HEAD_DIM), st(ik, bsz, seq, D_IDX),
            st(kb, bsz, seq, KVH_B, HEAD_DIM), st(vb, bsz, seq, KVH_B, HEAD_DIM), st(logf, bsz, seq, H_B),
            st(mk, bsz, N_MEM, H_C, HD_C), st(mv, bsz, N_MEM, H_C, HD_C),
            st(ka_s, dec, 1, KVH_A, HEAD_DIM), st(va_s, dec, 1, KVH_A, HEAD_DIM), st(ik_s, dec, 1, D_IDX),
            st(kb_s, dec, 1, KVH_B, HEAD_DIM), st(vb_s, dec, 1, KVH_B, HEAD_DIM), st(logf_s, dec, 1, H_B))
```

```python
import functools

import numpy as np
import jax
import jax.numpy as jnp
from jax import lax
from jax.experimental import pallas as pl
from jax.experimental.pallas import tpu as pltpu

D_MODEL = 4096
BATCH = 4
SEQ = 2048
DEC_BATCH = 32
PAST_LEN = 8192
PAGE_SIZE = 128
HEAD_DIM = 128
H_A = 12
KVH_A = 4
H_IDX = 16
D_IDX = 64
TOPK_MAX = 256
H_B = 12
KVH_B = 4
H_C = 4
HD_C = 256
N_MEM = 256
ROPE_THETA = 500000.0
ROT_FRAC = 4
EPS = 1e-6

F32 = jnp.float32
BF16 = jnp.bfloat16
NEG = -1e30
LANES = 128
HEAD_ROWS = 16
MIB = 1024 * 1024

W_A = H_A * HEAD_DIM
W_B = H_B * HEAD_DIM
W_C = H_C * HD_C
W_KV = KVH_A * HEAD_DIM
COL_TILE = 512
LF_LANE = D_IDX + H_IDX


def _layout():
    src, acc = {}, 0
    for n, s in zip(("qa", "ka", "va", "iq", "ik", "iw", "za", "qb", "kb", "vb", "fb", "zb", "qc", "zc", "gates"),
                    (W_A, W_KV, W_KV, H_IDX * D_IDX, D_IDX, H_IDX, W_A, W_B, W_KV, W_KV, H_B, W_B, W_C, W_C,
                     3 * D_MODEL)):
        src[n] = acc
        acc += s
    names = ("qa", "qb", "za", "zb", "ka", "va", "kb", "vb", "iq", "qc", "zc", "gates", "s1", "s2")
    sizes = (W_A, W_B, W_A, W_B, W_KV, W_KV, W_KV, W_KV, H_IDX * D_IDX, W_C, W_C, 3 * D_MODEL, COL_TILE, COL_TILE)
    starts = dict(src, s1=src["ik"], s2=src["fb"])
    off, tiles, acc = {}, [], 0
    for n, s in zip(names, sizes):
        off[n] = acc
        acc += s
        tiles += [starts[n] + COL_TILE * i for i in range(s // COL_TILE)]
    return off, acc, tiles


def _params(n_axes, vmem_mib):
    return pltpu.CompilerParams(dimension_semantics=("arbitrary",) * n_axes,
                                vmem_limit_bytes=vmem_mib * MIB)


def _sigmoid(z):
    return 1.0 / (1.0 + jnp.exp(-z))


def _silu(z):
    return z * _sigmoid(z)


def _log_sigmoid(z):
    return -(jnp.maximum(-z, 0.0) + jnp.log1p(jnp.exp(-jnp.abs(z))))


def _rot(x, c, sa, sb, half):
    n = x.shape[-1]
    return x * c + pltpu.roll(x, n - half, 1) * sa + pltpu.roll(x, half, 1) * sb


def _dot_nt(a, b):
    return lax.dot_general(a, b, (((1,), (1,)), ((), ())), preferred_element_type=F32)


def _split3(x):
    hi = x.astype(BF16)
    r = x - hi.astype(F32)
    mid = r.astype(BF16)
    lo = (r - mid.astype(F32)).astype(BF16)
    return hi, mid, lo


def _kth_largest_key(key, k):
    kf = jnp.float32(k)

    def count_ge(t):
        return jnp.sum((key >= t).astype(F32), axis=1, keepdims=True)

    int_min = jnp.int32(-2 ** 31)
    thr = jnp.where(count_ge(jnp.int32(0)) >= kf, jnp.int32(0), int_min)
    thr = jnp.broadcast_to(thr, (key.shape[0], 1)).astype(jnp.int32)

    def body(i, thr):
        cand = thr + jnp.left_shift(jnp.int32(1), jnp.int32(30) - i)
        return jnp.where(count_ge(cand) >= kf, cand, thr)

    return lax.fori_loop(0, 31, body, thr)


def _sortable_key(scores):
    bits = lax.bitcast_convert_type(scores + 0.0, jnp.int32)
    return jnp.where(bits < 0, bits ^ jnp.int32(0x7FFFFFFF), bits)


def _rmsnorm_body(x_ref, g_ref, o_ref):
    x = x_ref[...]
    ms = jnp.mean(x * x, axis=-1, keepdims=True)
    o_ref[...] = ((x * lax.rsqrt(ms + EPS)) * g_ref[...]).astype(o_ref.dtype)


def _rmsnorm(x, g, out_dtype, bm):
    m, d = x.shape
    bm = min(bm, m)
    return pl.pallas_call(
        _rmsnorm_body,
        grid=(m // bm,),
        in_specs=[pl.BlockSpec((bm, d), lambda i: (i, 0)), pl.BlockSpec((1, d), lambda i: (0, 0))],
        out_specs=pl.BlockSpec((bm, d), lambda i: (i, 0)),
        out_shape=jax.ShapeDtypeStruct((m, d), out_dtype),
        compiler_params=_params(1, 40),
        name="rmsnorm",
    )(x, g.reshape(1, d))


def _mm_body(x_ref, w_ref, o_ref):
    o_ref[...] = jnp.dot(x_ref[...], w_ref[...], preferred_element_type=F32)


def _mm_nt_body(x_ref, wt_ref, o_ref):
    o_ref[...] = _dot_nt(x_ref[...], wt_ref[...])


def _matmul_nt(x, wt, bm, bn, name):
    m, k = x.shape
    n = wt.shape[0]
    bm, bn = min(bm, m), min(bn, n)
    return pl.pallas_call(
        _mm_nt_body,
        grid=(m // bm, n // bn),
        in_specs=[pl.BlockSpec((bm, k), lambda i, j: (i, 0)), pl.BlockSpec((bn, k), lambda i, j: (j, 0))],
        out_specs=pl.BlockSpec((bm, bn), lambda i, j: (i, j)),
        out_shape=jax.ShapeDtypeStruct((m, n), F32),
        compiler_params=_params(2, 48),
        name=name,
    )(x, wt)


def _matmul(x, w, bm, bn, name):
    m, k = x.shape
    n = w.shape[1]
    bm, bn = min(bm, m), min(bn, n)
    return pl.pallas_call(
        _mm_body,
        grid=(m // bm, n // bn),
        in_specs=[pl.BlockSpec((bm, k), lambda i, j: (i, 0)), pl.BlockSpec((k, bn), lambda i, j: (0, j))],
        out_specs=pl.BlockSpec((bm, bn), lambda i, j: (i, j)),
        out_shape=jax.ShapeDtypeStruct((m, n), F32),
        compiler_params=_params(2, 48),
        name=name,
    )(x, w)


def _wprep_body(shifts, tab_ref, a_ref, b_ref, o_ref):
    shift = tab_ref[1, pl.program_id(1)]

    @pl.when(shift == 0)
    def _():
        o_ref[...] = a_ref[...].astype(o_ref.dtype)

    for sh in shifts:
        @pl.when(shift == sh)
        def _(sh=sh):
            x = jnp.concatenate([a_ref[...], b_ref[...]], axis=0).astype(F32)
            o_ref[...] = pltpu.roll(x, 2 * COL_TILE - sh, 0)[:COL_TILE, :].astype(o_ref.dtype)


def _wprep(wt, tiles, bk):
    n_src, k = wt.shape
    last = pl.cdiv(n_src, COL_TILE) - 1
    tab = jnp.asarray([[c // COL_TILE for c in tiles], [c % COL_TILE for c in tiles]], jnp.int32)
    shifts = tuple(sorted({c % COL_TILE for c in tiles} - {0}))
    bk = min(bk, k)
    grid_spec = pltpu.PrefetchScalarGridSpec(
        num_scalar_prefetch=1,
        grid=(k // bk, len(tiles)),
        in_specs=[pl.BlockSpec((COL_TILE, bk), lambda i, t, tab: (tab[0, t], i)),
                  pl.BlockSpec((COL_TILE, bk), lambda i, t, tab: (jnp.minimum(tab[0, t] + 1, last), i))],
        out_specs=pl.BlockSpec((COL_TILE, bk), lambda i, t, tab: (t, i)),
    )
    return pl.pallas_call(
        functools.partial(_wprep_body, shifts),
        grid_spec=grid_spec,
        out_shape=jax.ShapeDtypeStruct((COL_TILE * len(tiles), k), BF16),
        compiler_params=_params(2, 40),
        name="w_prep",
    )(tab, wt, wt)


def _cast_body(x_ref, o_ref):
    o_ref[...] = x_ref[...].astype(o_ref.dtype)


def _cast_bf16(w, bm):
    m, n = w.shape
    return pl.pallas_call(
        _cast_body,
        grid=(m // bm,),
        in_specs=[pl.BlockSpec((bm, n), lambda i: (i, 0))],
        out_specs=pl.BlockSpec((bm, n), lambda i: (i, 0)),
        out_shape=jax.ShapeDtypeStruct((m, n), BF16),
        compiler_params=_params(1, 40),
        name="w_cast",
    )(w)


def _post_body(ka_ref, s1_ref, s2_ref, tab_ref, bf_ref, ka_o, sm_o):
    c, sa, sb = tab_ref[0], tab_ref[1], tab_ref[2]
    for h in range(KVH_A):
        sl = slice(h * HEAD_DIM, (h + 1) * HEAD_DIM)
        ka_o[:, sl] = _rot(ka_ref[:, sl], c, sa, sb, HEAD_DIM // ROT_FRAC // 2)
    r = _rot(s1_ref[...], tab_ref[6], tab_ref[7], tab_ref[8], D_IDX // ROT_FRAC // 2)
    lf = _log_sigmoid(pltpu.roll(s2_ref[...], LF_LANE, 1) + bf_ref[...])
    lane = lax.broadcasted_iota(jnp.int32, r.shape, 1)
    sm_o[...] = jnp.where(lane < LF_LANE, r, jnp.where(lane < LF_LANE + H_B, lf, 0.0))


def _post(proj, tab, bf_row, seq_rows, bm):
    off = _layout()[0]
    m = proj.shape[0]
    bm = min(bm, m)
    tb = seq_rows // bm
    return pl.pallas_call(
        _post_body,
        grid=(m // bm,),
        in_specs=[pl.BlockSpec((bm, W_KV), lambda i: (i, off["ka"] // W_KV)),
                  pl.BlockSpec((bm, LANES), lambda i: (i, off["s1"] // LANES)),
                  pl.BlockSpec((bm, LANES), lambda i: (i, off["s2"] // LANES)),
                  pl.BlockSpec((9, bm, LANES), lambda i: (0, i % tb, 0)),
                  pl.BlockSpec((1, LANES), lambda i: (0, 0))],
        out_specs=[pl.BlockSpec((bm, W_KV), lambda i: (i, 0)), pl.BlockSpec((bm, LANES), lambda i: (i, 0))],
        out_shape=[jax.ShapeDtypeStruct((m, W_KV), F32), jax.ShapeDtypeStruct((m, LANES), F32)],
        compiler_params=_params(1, 40),
        name="kv_post",
    )(proj, proj, proj, tab, bf_row)


def _qrot_body(qa_ref, iq_ref, tab_ref, qa_o, iq_o):
    for h in range(H_A):
        sl = slice(h * HEAD_DIM, (h + 1) * HEAD_DIM)
        qa_o[:, sl] = _rot(qa_ref[:, sl], tab_ref[0], tab_ref[1], tab_ref[2], HEAD_DIM // ROT_FRAC // 2)
    for hp in range(H_IDX * D_IDX // LANES):
        sl = slice(hp * LANES, (hp + 1) * LANES)
        iq_o[:, sl] = _rot(iq_ref[:, sl], tab_ref[3], tab_ref[4], tab_ref[5], D_IDX // ROT_FRAC // 2)


def _qrot(proj, tab):
    off = _layout()[0]
    m = proj.shape[0]
    wi = H_IDX * D_IDX
    return pl.pallas_call(
        _qrot_body,
        grid=(1,),
        in_specs=[pl.BlockSpec((m, W_A), lambda i: (0, off["qa"] // W_A)),
                  pl.BlockSpec((m, wi), lambda i: (0, off["iq"] // wi)),
                  pl.BlockSpec((9, m, LANES), lambda i: (0, 0, 0))],
        out_specs=[pl.BlockSpec((m, W_A), lambda i: (0, 0)), pl.BlockSpec((m, wi), lambda i: (0, 0))],
        out_shape=[jax.ShapeDtypeStruct((m, W_A), F32), jax.ShapeDtypeStruct((m, wi), F32)],
        compiler_params=_params(1, 32),
        name="q_rotary",
    )(proj, proj, tab)


def _cumsum_body(sm_ref, ccol_ref, crow_ref):
    r = lax.broadcasted_iota(jnp.int32, (LANES, LANES), 0)
    c = lax.broadcasted_iota(jnp.int32, (LANES, LANES), 1)
    tri = (c <= r).astype(BF16)
    keep = (c >= LF_LANE) & (c < LF_LANE + H_B)
    carry = jnp.zeros((1, LANES), F32)
    for blk in range(ccol_ref.shape[0] // LANES):
        rows = slice(blk * LANES, (blk + 1) * LANES)
        x = jnp.where(keep, sm_ref[rows, :], 0.0)
        hi, mid, lo = _split3(x)
        cb = (jnp.dot(tri, hi, preferred_element_type=F32) + jnp.dot(tri, mid, preferred_element_type=F32)
              + jnp.dot(tri, lo, preferred_element_type=F32)) + carry
        carry = cb[LANES - 1:LANES, :]
        ccol_ref[rows, :] = cb
        per = crow_ref.shape[3] // LANES
        crow_ref[0, blk // per, :, (blk % per) * LANES:(blk % per + 1) * LANES] = cb.T


def _cumsum(small, batch, seq, tk):
    return pl.pallas_call(
        _cumsum_body,
        grid=(batch,),
        in_specs=[pl.BlockSpec((seq, LANES), lambda b: (b, 0))],
        out_specs=[pl.BlockSpec((seq, LANES), lambda b: (b, 0)),
                   pl.BlockSpec((1, seq // tk, LANES, tk), lambda b: (b, 0, 0, 0))],
        out_shape=[jax.ShapeDtypeStruct((batch * seq, LANES), F32),
                   jax.ShapeDtypeStruct((batch, seq // tk, LANES, tk), F32)],
        compiler_params=_params(1, 32),
        name="fox_cumsum",
    )(small)


def _softmax_av(s, v):
    m = jnp.max(s, axis=1, keepdims=True)
    p = jnp.exp(s - m)
    l = jnp.sum(p, axis=1, keepdims=True)
    return jnp.dot(p.astype(BF16), v, preferred_element_type=F32) / l


def _lane_pieces(x):
    return [x[:, t * LANES:(t + 1) * LANES] for t in range(x.shape[1] // LANES)]


def _lane_tile(x, width):
    return jnp.concatenate([x] * (width // LANES), axis=1)


def _attend_two_pass(n_heads, hd, qi, logits, values, last_mask, z_ref, u_ref, mx_ref, ls_ref, acc_ref):
    tq = tk = mx_ref.shape[1]

    def each_chunk(step):
        if last_mask is None:
            lax.fori_loop(0, qi + 1, lambda ci, c: (step(ci, None), c)[1], 0)
        else:
            lax.fori_loop(0, qi, lambda ci, c: (step(ci, None), c)[1], 0)
            step(qi, last_mask)

    def masked(ci, h, mask):
        s = logits(ci, h)
        return s if mask is None else s + mask

    mx_ref[...] = jnp.full(mx_ref.shape, NEG, F32)

    def row_max(ci, mask):
        for h in range(n_heads):
            m = mx_ref[h]
            for piece in _lane_pieces(masked(ci, h, mask)):
                m = jnp.maximum(m, piece)
            mx_ref[h] = m

    each_chunk(row_max)
    for h in range(n_heads):
        mx_ref[h] = jnp.broadcast_to(jnp.max(mx_ref[h], axis=1, keepdims=True), (tq, LANES))
    ls_ref[...] = jnp.zeros(ls_ref.shape, F32)
    acc_ref[...] = jnp.zeros(acc_ref.shape, F32)

    def accumulate(ci, mask):
        for h in range(n_heads):
            sl = slice(h * hd, (h + 1) * hd)
            p = jnp.exp(masked(ci, h, mask) - _lane_tile(mx_ref[h], tk))
            l = ls_ref[h]
            for piece in _lane_pieces(p):
                l = l + piece
            ls_ref[h] = l
            acc_ref[:, sl] += jnp.dot(p.astype(BF16), values(ci, h), preferred_element_type=F32)

    each_chunk(accumulate)
    for h in range(n_heads):
        sl = slice(h * hd, (h + 1) * hd)
        l = jnp.sum(ls_ref[h], axis=1, keepdims=True)
        u_ref[:, sl] = ((acc_ref[:, sl] / l) * _silu(z_ref[:, sl])).astype(u_ref.dtype)


def _chunk_rows(c, tk):
    return pl.ds(pl.multiple_of(c * tk, tk), tk)


def _dsa_prompt_body(topk, q_ref, iq_ref, sm_ref, z_ref, k_ref, v_ref, ik_ref, tab_ref, u_ref,
                     xb_ref, w_ref, key_ref, bias_ref, qb_ref, mx_ref, ls_ref, acc_ref):
    tq = q_ref.shape[0]
    qi = pl.program_id(1)
    r = lax.broadcasted_iota(jnp.int32, (tq, tq), 0)
    c = lax.broadcasted_iota(jnp.int32, (tq, tq), 1)
    tri = c <= r

    for hp in range(H_IDX * D_IDX // LANES):
        sl = slice(hp * LANES, (hp + 1) * LANES)
        xb_ref[:, sl] = _rot(iq_ref[:, sl], tab_ref[3], tab_ref[4], tab_ref[5], D_IDX // ROT_FRAC // 2).astype(BF16)
    for h in range(H_IDX):
        w_ref[h] = jnp.broadcast_to(sm_ref[:, D_IDX + h:D_IDX + h + 1] * (H_IDX ** -0.5), (tq, LANES))

    def scores(ci):
        rows = _chunk_rows(ci, tq)
        acc = jnp.zeros((tq, tq), F32)
        for hp in range(H_IDX * D_IDX // LANES):
            xb = xb_ref[:, hp * LANES:(hp + 1) * LANES]
            for j in range(LANES // D_IDX):
                h = hp * (LANES // D_IDX) + j
                s = _dot_nt(xb, ik_ref[rows, j * LANES:(j + 1) * LANES]) * (D_IDX ** -0.5)
                acc = acc + jnp.maximum(s, 0.0) * _lane_tile(w_ref[h], tq)
        return acc

    def fill_keys(ci, carry):
        key_ref[ci] = _sortable_key(scores(ci))
        return carry

    lax.fori_loop(0, qi, fill_keys, 0)
    key_ref[qi] = _sortable_key(jnp.where(tri, scores(qi), -jnp.inf))

    kf = jnp.float32(topk)

    def count_ge(cand):
        def add(ci, cnt):
            hit = jnp.where(key_ref[ci] >= cand, 1.0, 0.0)
            for piece in _lane_pieces(hit):
                cnt = cnt + piece
            return cnt
        cnt = lax.fori_loop(0, qi + 1, add, jnp.zeros((tq, LANES), F32))
        return jnp.sum(cnt, axis=1, keepdims=True)

    zero = jnp.zeros((tq, 1), jnp.int32)
    thr = jnp.where(count_ge(zero) >= kf, zero, jnp.int32(-2 ** 31))

    def refine(i, thr):
        cand = thr + jnp.left_shift(jnp.int32(1), jnp.int32(30) - i)
        return jnp.where(count_ge(cand) >= kf, cand, thr)

    thr = lax.fori_loop(0, 31, refine, thr)

    need = kf - count_ge(thr + 1)
    prefix = (r <= c).astype(BF16)

    def keep_bias(ci, run, mask):
        kk = key_ref[ci]
        tie = kk == thr
        tie_f = jnp.where(tie, 1.0, 0.0)
        seen = jnp.dot(tie_f.astype(BF16), prefix, preferred_element_type=F32) + run
        keep = (kk > thr) | (tie & (seen <= need))
        if mask is not None:
            keep = keep & mask
        bias_ref[ci] = jnp.where(keep, 0.0, NEG)
        return run + jnp.sum(tie_f, axis=1, keepdims=True)

    run = lax.fori_loop(0, qi, lambda ci, run: keep_bias(ci, run, None), jnp.zeros((tq, 1), F32))
    keep_bias(qi, run, tri)

    grp = H_A // KVH_A
    for h in range(H_A):
        sl = slice(h * HEAD_DIM, (h + 1) * HEAD_DIM)
        qb_ref[:, sl] = _rot(q_ref[:, sl], tab_ref[0], tab_ref[1], tab_ref[2],
                             HEAD_DIM // ROT_FRAC // 2).astype(BF16)

    def kv_cols(h):
        return slice(h // grp * HEAD_DIM, (h // grp + 1) * HEAD_DIM)

    def logits(ci, h):
        qh = qb_ref[:, h * HEAD_DIM:(h + 1) * HEAD_DIM]
        return _dot_nt(qh, k_ref[_chunk_rows(ci, tq), kv_cols(h)]) * (HEAD_DIM ** -0.5) + bias_ref[ci]

    def values(ci, h):
        return v_ref[_chunk_rows(ci, tq), kv_cols(h)]

    _attend_two_pass(H_A, HEAD_DIM, qi, logits, values, None, z_ref, u_ref, mx_ref, ls_ref, acc_ref)


def _dsa_prompt(proj, small, ka_bf, va_bf, ik_bf, tab, batch, seq, tq):
    off = _layout()[0]
    nq = seq // tq
    wi = H_IDX * D_IDX
    topk = min(TOPK_MAX, seq // 4)
    row = lambda b, q: b * nq + q
    return pl.pallas_call(
        functools.partial(_dsa_prompt_body, topk),
        grid=(batch, nq),
        scratch_shapes=[pltpu.VMEM((tq, wi), BF16), pltpu.VMEM((H_IDX, tq, LANES), F32),
                        pltpu.VMEM((nq, tq, tq), jnp.int32), pltpu.VMEM((nq, tq, tq), F32),
                        pltpu.VMEM((tq, W_A), BF16), pltpu.VMEM((H_A, tq, LANES), F32),
                        pltpu.VMEM((H_A, tq, LANES), F32), pltpu.VMEM((tq, W_A), F32)],
        in_specs=[pl.BlockSpec((tq, W_A), lambda b, q: (row(b, q), off["qa"] // W_A)),
                  pl.BlockSpec((tq, wi), lambda b, q: (row(b, q), off["iq"] // wi)),
                  pl.BlockSpec((tq, LANES), lambda b, q: (row(b, q), 0)),
                  pl.BlockSpec((tq, W_A), lambda b, q: (row(b, q), off["za"] // W_A)),
                  pl.BlockSpec((seq, W_KV), lambda b, q: (b, 0)),
                  pl.BlockSpec((seq, W_KV), lambda b, q: (b, 0)),
                  pl.BlockSpec((seq, 2 * LANES), lambda b, q: (b, 0)),
                  pl.BlockSpec((9, tq, LANES), lambda b, q: (0, q, 0))],
        out_specs=pl.BlockSpec((tq, W_A), lambda b, q: (row(b, q), 0)),
        out_shape=jax.ShapeDtypeStruct((batch * seq, W_A), BF16),
        compiler_params=_params(2, 56),
        name="dsa_prompt",
    )(proj, proj, small, proj, ka_bf, va_bf, ik_bf, tab)


def _fox_prompt_body(q_ref, z_ref, k_ref, v_ref, ccol_ref, crow_ref, u_ref, qb_ref, cq_ref, mx_ref, ls_ref, acc_ref):
    tq = q_ref.shape[0]
    qi = pl.program_id(1)
    r = lax.broadcasted_iota(jnp.int32, (tq, tq), 0)
    c = lax.broadcasted_iota(jnp.int32, (tq, tq), 1)
    diag_bias = jnp.where(c <= r, 0.0, NEG)
    grp = H_B // KVH_B
    qb_ref[...] = q_ref[...].astype(BF16)
    for h in range(H_B):
        cq_ref[h] = jnp.broadcast_to(ccol_ref[:, LF_LANE + h:LF_LANE + h + 1], (tq, LANES))

    def kv_cols(h):
        return slice(h // grp * HEAD_DIM, (h // grp + 1) * HEAD_DIM)

    def logits(ci, h):
        qh = qb_ref[:, h * HEAD_DIM:(h + 1) * HEAD_DIM]
        s = _dot_nt(qh, k_ref[_chunk_rows(ci, tq), kv_cols(h)]) * (HEAD_DIM ** -0.5)
        return s + _lane_tile(cq_ref[h], tq) - crow_ref[0, ci, LF_LANE + h:LF_LANE + h + 1, :]

    def values(ci, h):
        return v_ref[_chunk_rows(ci, tq), kv_cols(h)]

    _attend_two_pass(H_B, HEAD_DIM, qi, logits, values, diag_bias, z_ref, u_ref, mx_ref, ls_ref, acc_ref)


def _fox_prompt(proj, kb_bf, vb_bf, ccol, crow, batch, seq, tq):
    off = _layout()[0]
    nq = seq // tq
    row = lambda b, q: b * nq + q
    return pl.pallas_call(
        _fox_prompt_body,
        grid=(batch, nq),
        in_specs=[pl.BlockSpec((tq, W_B), lambda b, q: (row(b, q), off["qb"] // W_B)),
                  pl.BlockSpec((tq, W_B), lambda b, q: (row(b, q), off["zb"] // W_B)),
                  pl.BlockSpec((seq, W_KV), lambda b, q: (b, 0)),
                  pl.BlockSpec((seq, W_KV), lambda b, q: (b, 0)),
                  pl.BlockSpec((tq, LANES), lambda b, q: (row(b, q), 0)),
                  pl.BlockSpec((1, nq, LANES, tq), lambda b, q: (b, 0, 0, 0))],
        out_specs=pl.BlockSpec((tq, W_B), lambda b, q: (row(b, q), 0)),
        scratch_shapes=[pltpu.VMEM((tq, W_B), BF16), pltpu.VMEM((H_B, tq, LANES), F32),
                        pltpu.VMEM((H_B, tq, LANES), F32), pltpu.VMEM((H_B, tq, LANES), F32),
                        pltpu.VMEM((tq, W_B), F32)],
        out_shape=jax.ShapeDtypeStruct((batch * seq, W_B), BF16),
        compiler_params=_params(2, 56),
        name="fox_prompt",
    )(proj, proj, kb_bf, vb_bf, ccol, crow)


def _cross_prompt_body(q_ref, z_ref, k_ref, v_ref, u_ref):
    for h in range(H_C):
        sl = slice(h * HD_C, (h + 1) * HD_C)
        s = _dot_nt(q_ref[:, sl].astype(BF16), k_ref[:, sl]) * (HD_C ** -0.5)
        o = _softmax_av(s, v_ref[:, sl])
        u_ref[:, sl] = (o * _silu(z_ref[:, sl])).astype(u_ref.dtype)


def _cross_prompt(proj, mk_bf, mv_bf, batch, seq, tq):
    off = _layout()[0]
    nq = seq // tq
    row = lambda b, q: b * nq + q
    return pl.pallas_call(
        _cross_prompt_body,
        grid=(batch, nq),
        in_specs=[pl.BlockSpec((tq, W_C), lambda b, q: (row(b, q), off["qc"] // W_C)),
                  pl.BlockSpec((tq, W_C), lambda b, q: (row(b, q), off["zc"] // W_C)),
                  pl.BlockSpec((N_MEM, W_C), lambda b, q: (b, 0)),
                  pl.BlockSpec((N_MEM, W_C), lambda b, q: (b, 0))],
        out_specs=pl.BlockSpec((tq, W_C), lambda b, q: (row(b, q), 0)),
        out_shape=jax.ShapeDtypeStruct((batch * seq, W_C), BF16),
        compiler_params=_params(2, 40),
        name="cross_prompt",
    )(proj, proj, mk_bf, mv_bf)


def _merge_body(ua_ref, ub_ref, uc_ref, wa_ref, wb_ref, wc_ref, ga_ref, gb_ref, gc_ref, h_ref):
    h = (_sigmoid(ga_ref[...]) * jnp.dot(ua_ref[...], wa_ref[...], preferred_element_type=F32)
         + _sigmoid(gb_ref[...]) * jnp.dot(ub_ref[...], wb_ref[...], preferred_element_type=F32)
         + _sigmoid(gc_ref[...]) * jnp.dot(uc_ref[...], wc_ref[...], preferred_element_type=F32))
    h_ref[...] = h.astype(h_ref.dtype)


def _merge(ua, ub, uc, wa, wb, wc, proj, bm, bn):
    off = _layout()[0]
    m = ua.shape[0]
    d = wa.shape[1]
    bm, bn = min(bm, m), min(bn, d)
    g0 = off["gates"] // bn
    nd = d // bn
    return pl.pallas_call(
        _merge_body,
        grid=(m // bm, nd),
        in_specs=[pl.BlockSpec((bm, W_A), lambda i, j: (i, 0)),
                  pl.BlockSpec((bm, W_B), lambda i, j: (i, 0)),
                  pl.BlockSpec((bm, W_C), lambda i, j: (i, 0)),
                  pl.BlockSpec((W_A, bn), lambda i, j: (0, j)),
                  pl.BlockSpec((W_B, bn), lambda i, j: (0, j)),
                  pl.BlockSpec((W_C, bn), lambda i, j: (0, j)),
                  pl.BlockSpec((bm, bn), lambda i, j: (i, g0 + j)),
                  pl.BlockSpec((bm, bn), lambda i, j: (i, g0 + nd + j)),
                  pl.BlockSpec((bm, bn), lambda i, j: (i, g0 + 2 * nd + j))],
        out_specs=pl.BlockSpec((bm, bn), lambda i, j: (i, j)),
        out_shape=jax.ShapeDtypeStruct((m, d), BF16),
        compiler_params=_params(2, 48),
        name="merge",
    )(ua, ub, uc, wa, wb, wc, proj, proj, proj)


def _outproj_body(x_ref, h_ref, w_ref, g_ref, o_ref, y_ref):
    j = pl.program_id(1)
    nj = pl.num_programs(1)
    bn = w_ref.shape[1]
    y_ref[j] = x_ref[...] + jnp.dot(h_ref[...], w_ref[...], preferred_element_type=F32)

    @pl.when(j == nj - 1)
    def _():
        n_tiles = y_ref.shape[0]
        ss = jnp.zeros((y_ref.shape[1], 1), F32)
        for t in range(n_tiles):
            y = y_ref[t]
            ss = ss + jnp.sum(y * y, axis=-1, keepdims=True)
        inv = lax.rsqrt(ss * (1.0 / (n_tiles * bn)) + EPS)
        for t in range(n_tiles):
            o_ref[:, t * bn:(t + 1) * bn] = (y_ref[t] * inv) * g_ref[:, t * bn:(t + 1) * bn]


def _outproj_norm(x, h, w, g, bm, bn):
    m, d = x.shape
    bm, bn = min(bm, m), min(bn, d)
    return pl.pallas_call(
        _outproj_body,
        grid=(m // bm, d // bn),
        in_specs=[pl.BlockSpec((bm, bn), lambda i, j: (i, j)),
                  pl.BlockSpec((bm, d), lambda i, j: (i, 0)),
                  pl.BlockSpec((d, bn), lambda i, j: (0, j)),
                  pl.BlockSpec((1, d), lambda i, j: (0, 0))],
        out_specs=pl.BlockSpec((bm, d), lambda i, j: (i, 0)),
        out_shape=jax.ShapeDtypeStruct((m, d), F32),
        scratch_shapes=[pltpu.VMEM((d // bn, bm, bn), F32)],
        compiler_params=_params(2, 52),
        name="out_proj",
    )(x, h, w, g.reshape(1, d))


def _idx_sample_body(pps, pt_ref, iq_ref, w_ref, ikn_ref, *rest):
    pages = rest[:pps]
    sc_ref, own_ref = rest[pps], rest[pps + 1]
    iq = iq_ref[0]
    w = w_ref[0] * (H_IDX ** -0.5)
    for i in range(pps):
        s = jnp.dot(iq, pages[i][...].astype(BF16), preferred_element_type=F32) * (D_IDX ** -0.5)
        sc_ref[0, :, i * PAGE_SIZE:(i + 1) * PAGE_SIZE] = jnp.sum(jnp.maximum(s, 0.0) * w, axis=0, keepdims=True)

    @pl.when(pl.program_id(1) == 0)
    def _():
        kn = ikn_ref[0].astype(F32)
        s = jnp.sum(iq.astype(F32) * kn, axis=1, keepdims=True) * (D_IDX ** -0.5)
        own = jnp.sum(jnp.maximum(s, 0.0) * w, axis=0, keepdims=True)
        lane = lax.broadcasted_iota(jnp.int32, (1, LANES), 1)
        own_ref[0] = jnp.where(lane == 0, own, -jnp.inf)


def _idx_sample(page_table, iq_bf, iw, ik_new_bf, pool_idx, pps):
    b, n_pages = page_table.shape
    npc = n_pages // pps
    page_specs = [pl.BlockSpec((None, None, D_IDX, PAGE_SIZE),
                               lambda bi, pc, pt, i=i: (0, pt[bi, pc * pps + i], 0, 0)) for i in range(pps)]
    grid_spec = pltpu.PrefetchScalarGridSpec(
        num_scalar_prefetch=1,
        grid=(b, npc),
        in_specs=[pl.BlockSpec((1, H_IDX, D_IDX), lambda bi, pc, pt: (bi, 0, 0)),
                  pl.BlockSpec((1, H_IDX, 1), lambda bi, pc, pt: (bi, 0, 0)),
                  pl.BlockSpec((1, 1, D_IDX), lambda bi, pc, pt: (bi, 0, 0))] + page_specs,
        out_specs=[pl.BlockSpec((1, 1, pps * PAGE_SIZE), lambda bi, pc, pt: (bi, 0, pc)),
                   pl.BlockSpec((1, 1, LANES), lambda bi, pc, pt: (bi, 0, 0))],
    )
    return pl.pallas_call(
        functools.partial(_idx_sample_body, pps),
        grid_spec=grid_spec,
        out_shape=[jax.ShapeDtypeStruct((b, 1, n_pages * PAGE_SIZE), F32),
                   jax.ShapeDtypeStruct((b, 1, LANES), F32)],
        compiler_params=_params(2, 32),
        name="idx_sample",
    )(page_table, iq_bf, iw, ik_new_bf, *([pool_idx] * pps))


def _topk_sample_body(topk, n_valid, grp, sc_ref, bias_ref):
    sc = sc_ref[...]
    lane = lax.broadcasted_iota(jnp.int32, sc.shape, 1)
    valid = lane < n_valid
    key = _sortable_key(jnp.where(valid, sc, -jnp.inf))
    thr = _kth_largest_key(key, topk)
    need = jnp.float32(topk) - jnp.sum(jnp.where(key > thr, 1.0, 0.0), axis=1, keepdims=True)
    pr = lax.broadcasted_iota(jnp.int32, (LANES, LANES), 0)
    pc = lax.broadcasted_iota(jnp.int32, (LANES, LANES), 1)
    prefix = (pr <= pc).astype(BF16)
    k = lax.broadcasted_iota(jnp.int32, (LANES, LANES * grp), 0)
    r = lax.broadcasted_iota(jnp.int32, (LANES, LANES * grp), 1)
    rep = ((r >= k * grp) & (r < (k + 1) * grp)).astype(BF16)
    run = jnp.zeros((sc.shape[0], 1), F32)
    for c in range(sc.shape[1] // LANES):
        sl = slice(c * LANES, (c + 1) * LANES)
        kk = key[:, sl]
        tie = kk == thr
        tie_f = jnp.where(tie, 1.0, 0.0)
        seen = jnp.dot(tie_f.astype(BF16), prefix, preferred_element_type=F32) + run
        in_range = lax.broadcasted_iota(jnp.int32, kk.shape, 1) < n_valid - c * LANES
        keep = ((kk > thr) | (tie & (seen <= need))) & in_range
        flags = jnp.dot(jnp.where(keep, 1.0, 0.0).astype(BF16), rep, preferred_element_type=F32)
        bias_ref[:, c * LANES * grp:(c + 1) * LANES * grp] = jnp.where(flags > 0.5, 0.0, NEG)
        run = run + jnp.sum(tie_f, axis=1, keepdims=True)


def _topk_sample(scores, n_valid, grp):
    topk = min(TOPK_MAX, n_valid // 4)
    b, n = scores.shape
    return pl.pallas_call(
        functools.partial(_topk_sample_body, topk, n_valid, grp),
        grid=(1,),
        in_specs=[pl.BlockSpec((b, n), lambda i: (0, 0))],
        out_specs=pl.BlockSpec((b, n * grp), lambda i: (0, 0)),
        out_shape=jax.ShapeDtypeStruct((b, n * grp), F32),
        compiler_params=_params(1, 40),
        name="topk_sample",
    )(scores)


def _fox_bias_body(pps, grp, pt_ref, lfn_ref, *rest):
    pages = rest[:pps]
    d_ref, x_ref, carry_ref = rest[pps], rest[pps + 1], rest[pps + 2]
    rows = PAGE_SIZE * grp
    j = lax.broadcasted_iota(jnp.int32, (LANES, rows), 0)
    r = lax.broadcasted_iota(jnp.int32, (LANES, rows), 1)
    upper = (r < j * grp).astype(BF16)

    @pl.when(pl.program_id(1) == 0)
    def _():
        x_ref[...] = jnp.zeros_like(x_ref)
        x_ref[0, 0:1, :] = lfn_ref[0]
        carry_ref[...] = x_ref[0].T[0:HEAD_ROWS, 0:1]
        x_ref[0, 0:1, :] = jnp.zeros((1, LANES), F32)

    carry = carry_ref[...]
    for i in reversed(range(pps)):
        x_ref[i, :, 0:H_B] = pages[i][...]
        xt = x_ref[i].T[0:HEAD_ROWS, :]
        hi, mid, lo = _split3(xt)
        d_loc = (jnp.dot(hi, upper, preferred_element_type=F32) + jnp.dot(mid, upper, preferred_element_type=F32)
                 + jnp.dot(lo, upper, preferred_element_type=F32))
        d_ref[0, :, i * rows:(i + 1) * rows] = d_loc + carry
        carry = carry + jnp.sum(xt, axis=1, keepdims=True)
    carry_ref[...] = carry


def _fox_bias(page_table, lf_new, pool_lf, pps, grp):
    b, n_pages = page_table.shape
    npc = n_pages // pps
    rows = PAGE_SIZE * grp
    page_specs = [pl.BlockSpec((None, None, PAGE_SIZE, H_B),
                               lambda bi, pc, pt, i=i: (0, pt[bi, (npc - 1 - pc) * pps + i], 0, 0))
                  for i in range(pps)]
    grid_spec = pltpu.PrefetchScalarGridSpec(
        num_scalar_prefetch=1,
        grid=(b, npc),
        in_specs=[pl.BlockSpec((1, 1, LANES), lambda bi, pc, pt: (bi, 0, 0))] + page_specs,
        out_specs=pl.BlockSpec((1, HEAD_ROWS, pps * rows), lambda bi, pc, pt: (bi, 0, npc - 1 - pc)),
        scratch_shapes=[pltpu.VMEM((pps, LANES, LANES), F32), pltpu.VMEM((HEAD_ROWS, 1), F32)],
    )
    return pl.pallas_call(
        functools.partial(_fox_bias_body, pps, grp),
        grid_spec=grid_spec,
        out_shape=jax.ShapeDtypeStruct((b, HEAD_ROWS, n_pages * rows), F32),
        compiler_params=_params(2, 32),
        name="fox_bias_sample",
    )(page_table, lf_new, *([pool_lf] * pps))


def _paged_body(pps, scale, pt_ref, q_ref, bias_ref, gm_ref, kn_ref, vn_ref, nb_ref, z_ref, *rest):
    kps, vps = rest[:pps], rest[pps:2 * pps]
    u_ref, m_ref, l_ref, acc_ref = rest[2 * pps:2 * pps + 4]
    pc = pl.program_id(1)
    rows = kps[0].shape[0]
    q = q_ref[0]

    @pl.when(pc == 0)
    def _():
        kn = kn_ref[0].astype(BF16).astype(F32)
        vn = vn_ref[0].astype(BF16).astype(F32)
        nb = nb_ref[0]
        on = nb > 0.5 * NEG
        s0 = jnp.sum(q.astype(F32) * kn, axis=1, keepdims=True) * scale + nb
        m_ref[...] = jnp.where(on, s0, NEG)
        l_ref[...] = jnp.where(on, 1.0, 0.0)
        acc_ref[...] = jnp.where(on, vn, 0.0)

    ss = [(_dot_nt(q, kps[i][...].astype(BF16)) * scale + bias_ref[0, :, i * rows:(i + 1) * rows]) + gm_ref[...]
          for i in range(pps)]
    m_old = m_ref[...]
    m_new = m_old
    for s in ss:
        m_new = jnp.maximum(m_new, jnp.max(s, axis=1, keepdims=True))
    alpha = jnp.exp(m_old - m_new)
    l_new = alpha * l_ref[...]
    acc = alpha * acc_ref[...]
    for i, s in enumerate(ss):
        pf = jnp.where(s > 0.5 * NEG, jnp.exp(s - m_new), 0.0)
        l_new = l_new + jnp.sum(pf, axis=1, keepdims=True)
        acc = acc + jnp.dot(pf.astype(BF16), vps[i][...].astype(BF16), preferred_element_type=F32)
    l_ref[...] = l_new
    acc_ref[...] = acc
    m_ref[...] = m_new

    @pl.when(pc == pl.num_programs(1) - 1)
    def _():
        u_ref[0] = (acc_ref[...] / l_ref[...]) * _silu(z_ref[0])


def _paged_attn(page_idx, q, bias, gmask, k_new, v_new, new_bias, z, pool_k, pool_v, pps, scale, name, paged=True):
    b, n_pages = page_idx.shape
    rows, hd = (pool_k.shape[1] if paged else pool_k.shape[1] // n_pages), pool_k.shape[2]
    rb = bias.shape[1]
    npc = n_pages // pps
    if paged:
        kspecs = [pl.BlockSpec((None, rows, hd), lambda bi, pc, pt, i=i: (pt[bi, pc * pps + i], 0, 0))
                  for i in range(pps)]
    else:
        kspecs = [pl.BlockSpec((None, rows, hd), lambda bi, pc, pt, i=i: (bi, pc * pps + i, 0)) for i in range(pps)]
    per_b = lambda bi, pc, pt: (bi, 0, 0)
    head_blk = pl.BlockSpec((1, HEAD_ROWS, hd), per_b)
    grid_spec = pltpu.PrefetchScalarGridSpec(
        num_scalar_prefetch=1,
        grid=(b, npc),
        in_specs=[head_blk,
                  pl.BlockSpec((1, rb, pps * rows), lambda bi, pc, pt: (bi, 0, pc)),
                  pl.BlockSpec((HEAD_ROWS, rows), lambda bi, pc, pt: (0, 0)),
                  head_blk, head_blk,
                  pl.BlockSpec((1, HEAD_ROWS, 1), per_b),
                  head_blk] + kspecs + kspecs,
        out_specs=head_blk,
        scratch_shapes=[pltpu.VMEM((HEAD_ROWS, 1), F32), pltpu.VMEM((HEAD_ROWS, 1), F32),
                        pltpu.VMEM((HEAD_ROWS, hd), F32)],
    )
    return pl.pallas_call(
        functools.partial(_paged_body, pps, scale),
        grid_spec=grid_spec,
        out_shape=jax.ShapeDtypeStruct((b, HEAD_ROWS, hd), F32),
        compiler_params=_params(2, 48),
        name=name,
    )(page_idx, q, bias, gmask, k_new, v_new, new_bias, z, *([pool_k] * pps), *([pool_v] * pps))


def _rotary_tables(pos):
    def base(width):
        rot = width // ROT_FRAC
        half = rot // 2
        inv = jnp.power(jnp.float32(ROPE_THETA), -jnp.arange(half, dtype=F32) * (2.0 / rot))
        ang = pos.astype(F32)[:, None] * inv[None, :]
        cos, sin = jnp.cos(ang), jnp.sin(ang)
        n = pos.shape[0]
        ones, zeros = jnp.ones((n, width - rot), F32), jnp.zeros((n, width - rot), F32)
        zh = jnp.zeros((n, half), F32)
        return (jnp.concatenate([cos, cos, ones], 1), jnp.concatenate([-sin, zh, zeros], 1),
                jnp.concatenate([zh, sin, zeros], 1))

    c1, a1, b1 = base(HEAD_DIM)
    c2, a2, b2 = base(D_IDX)
    n = pos.shape[0]
    pad1, pad0 = jnp.ones((n, LANES - D_IDX), F32), jnp.zeros((n, LANES - D_IDX), F32)
    return jnp.stack([c1, a1, b1,
                      jnp.concatenate([c2, c2], 1), jnp.concatenate([a2, a2], 1), jnp.concatenate([b2, b2], 1),
                      jnp.concatenate([c2, pad1], 1), jnp.concatenate([a2, pad0], 1), jnp.concatenate([b2, pad0], 1)])


def _head_rows(x, n_heads, hd):
    b = x.shape[0]
    return jnp.pad(x.reshape(b, n_heads, hd), ((0, 0), (0, HEAD_ROWS - n_heads), (0, 0)))


def _group_rows(x, n_heads, n_groups, hd):
    b = x.shape[0]
    gid = np.minimum(np.arange(HEAD_ROWS), n_heads - 1) // (n_heads // n_groups)
    return x.reshape(b, n_groups, hd)[:, gid]


def _group_mask(n_heads, n_groups):
    gid = np.minimum(np.arange(HEAD_ROWS), n_heads - 1) // (n_heads // n_groups)
    r = np.arange(PAGE_SIZE * n_groups)
    return jnp.asarray(np.where((r[None, :] % n_groups) == gid[:, None], 0.0, NEG), F32)


def _trunk(x2d, proj, ua, ub, uc, w_a, w_b, w_c, w_o, g_final):
    h = _merge(ua, ub, uc, w_a, w_b, w_c, proj, 1024, 512)
    return _outproj_norm(x2d, h, w_o, g_final, 512, 512)


def kernel(x_prompt, x_sample, cache_a_k, cache_a_v, cache_a_idx, cache_b_k, cache_b_v, cache_b_logf, cache_mem_k,
           cache_mem_v, page_table, mem_prompt, g_norm, w_in, b_forget, w_br_a, w_br_b, w_br_c, w_out, g_mem,
           w_mem_kv, g_final):
    off, _, tiles = _layout()
    bsz, seq, d = x_prompt.shape
    dec = x_sample.shape[0]
    tq = min(256, seq)

    w_in_t = _wprep(jnp.transpose(w_in[0].astype(BF16)), tiles, 4096)
    w_a, w_b, w_c, w_o, w_m = (_cast_bf16(w[0], 256) for w in (w_br_a, w_br_b, w_br_c, w_out, w_mem_kv))
    bf_row = jnp.zeros((1, LANES), F32).at[0, LF_LANE:LF_LANE + H_B].set(b_forget[0])

    xp = x_prompt.reshape(bsz * seq, d)
    proj = _matmul_nt(_rmsnorm(xp, g_norm[0], BF16, 256), w_in_t, 1024, 512, "in_proj")
    tab_p = _rotary_tables(jnp.arange(seq, dtype=jnp.int32))
    ka_rot, small = _post(proj, tab_p, bf_row, seq, 512)
    col = lambda name, width: proj[:, off[name]:off[name] + width]
    va, kb, vb = col("va", W_KV), col("kb", W_KV), col("vb", W_KV)
    ik = small[:, :D_IDX]
    logf = small[:, LF_LANE:LF_LANE + H_B]

    mem_n = _rmsnorm(mem_prompt.reshape(bsz * N_MEM, d), g_mem[0], BF16, 256)
    mkv = _matmul(mem_n, w_m, 1024, 1024, "mem_kv")
    mk, mv = mkv[:, :W_C], mkv[:, W_C:]

    ik_bf = ik.astype(BF16)
    ik_pad = jnp.zeros_like(ik_bf)
    ik2 = jnp.concatenate([ik_bf, ik_pad, ik_pad, ik_bf], axis=1)
    ua = _dsa_prompt(proj, small, ka_rot.astype(BF16), va.astype(BF16), ik2, tab_p, bsz, seq, tq)
    ccol, crow = _cumsum(small, bsz, seq, tq)
    ub = _fox_prompt(proj, kb.astype(BF16), vb.astype(BF16), ccol, crow, bsz, seq, tq)
    uc = _cross_prompt(proj, mk.astype(BF16), mv.astype(BF16), bsz, seq, tq)
    y_prompt = _trunk(xp, proj, ua, ub, uc, w_a, w_b, w_c, w_o, g_final).reshape(bsz, seq, d)

    xs = x_sample.reshape(dec, d)
    proj_s = _matmul_nt(_rmsnorm(xs, g_norm[0], BF16, dec), w_in_t, dec, 512, "in_proj_sample")
    tab_s = _rotary_tables(jnp.full((dec,), page_table.shape[1] * PAGE_SIZE, jnp.int32))
    ka_s, small_s = _post(proj_s, tab_s, bf_row, dec, dec)
    cols = lambda name, width: proj_s[:, off[name]:off[name] + width]
    va_s, kb_s, vb_s = cols("va", W_KV), cols("kb", W_KV), cols("vb", W_KV)
    ik_s = small_s[:, :D_IDX]
    iw_s = small_s[:, D_IDX:D_IDX + H_IDX]
    logf_s = small_s[:, LF_LANE:LF_LANE + H_B]
    qa_s, iq_s = _qrot(proj_s, tab_s)

    n_pages = page_table.shape[1]
    pps = min(16, n_pages)
    pps_small = min(32, n_pages)
    scores, own = _idx_sample(page_table, iq_s.astype(BF16).reshape(dec, H_IDX, D_IDX), iw_s.reshape(dec, H_IDX, 1),
                              ik_s.astype(BF16).reshape(dec, 1, D_IDX), jnp.transpose(cache_a_idx, (0, 1, 3, 2)),
                              pps_small)
    n_keys = n_pages * PAGE_SIZE
    n_pool = cache_a_k.shape[1]
    flat = lambda pool, g, hd: pool.reshape(n_pool, PAGE_SIZE * g, hd)
    sel_bias = _topk_sample(jnp.concatenate([scores[:, 0], own[:, 0]], axis=1), n_keys + 1, KVH_A)

    scale = HEAD_DIM ** -0.5
    ua_full = _paged_attn(
        page_table, _head_rows(qa_s, H_A, HEAD_DIM).astype(BF16), sel_bias[:, None, :n_keys * KVH_A],
        _group_mask(H_A, KVH_A), _group_rows(ka_s, H_A, KVH_A, HEAD_DIM), _group_rows(va_s, H_A, KVH_A, HEAD_DIM),
        jnp.broadcast_to(sel_bias[:, None, n_keys * KVH_A:n_keys * KVH_A + 1], (dec, HEAD_ROWS, 1)),
        _head_rows(cols("za", W_A), H_A, HEAD_DIM),
        flat(cache_a_k, KVH_A, HEAD_DIM), flat(cache_a_v, KVH_A, HEAD_DIM), pps, scale, "dsa_sample")

    lf_new = jnp.pad(logf_s, ((0, 0), (0, LANES - H_B))).reshape(dec, 1, LANES)
    d_bias = _fox_bias(page_table, lf_new, cache_b_logf, min(16, n_pages), KVH_B)
    ub_full = _paged_attn(
        page_table, _head_rows(cols("qb", W_B), H_B, HEAD_DIM).astype(BF16), d_bias,
        _group_mask(H_B, KVH_B), _group_rows(kb_s, H_B, KVH_B, HEAD_DIM), _group_rows(vb_s, H_B, KVH_B, HEAD_DIM),
        jnp.zeros((dec, HEAD_ROWS, 1), F32), _head_rows(cols("zb", W_B), H_B, HEAD_DIM),
        flat(cache_b_k, KVH_B, HEAD_DIM), flat(cache_b_v, KVH_B, HEAD_DIM), pps, scale, "fox_sample")

    mem_pages = N_MEM // PAGE_SIZE
    uc_full = _paged_attn(
        jnp.zeros((dec, mem_pages), jnp.int32), _head_rows(cols("qc", W_C), H_C, HD_C).astype(BF16),
        jnp.zeros((dec, 1, N_MEM * H_C), F32), _group_mask(H_C, H_C),
        jnp.zeros((dec, HEAD_ROWS, HD_C), F32), jnp.zeros((dec, HEAD_ROWS, HD_C), F32),
        jnp.full((dec, HEAD_ROWS, 1), NEG, F32), _head_rows(cols("zc", W_C), H_C, HD_C),
        cache_mem_k.reshape(dec, N_MEM * H_C, HD_C), cache_mem_v.reshape(dec, N_MEM * H_C, HD_C),
        mem_pages, HD_C ** -0.5, "cross_sample", paged=False)

    ua_s = ua_full[:, :H_A].reshape(dec, W_A).astype(BF16)
    ub_s = ub_full[:, :H_B].reshape(dec, W_B).astype(BF16)
    uc_s = uc_full[:, :H_C].reshape(dec, W_C).astype(BF16)
    y_sample = _trunk(xs, proj_s, ua_s, ub_s, uc_s, w_a, w_b, w_c, w_o, g_final).reshape(dec, 1, d)

    st = lambda a, b, t, *tail: a.reshape((1, b, t) + tail)
    return (y_prompt, y_sample,
            st(ka_rot, bsz, seq, KVH_A, HEAD_DIM), st(va, bsz, seq, KVH_A, HEAD_DIM), st(ik, bsz, seq, D_IDX),
            st(kb, bsz, seq, KVH_B, HEAD_DIM), st(vb, bsz, seq, KVH_B, HEAD_DIM), st(logf, bsz, seq, H_B),
            st(mk, bsz, N_MEM, H_C, HD_C), st(mv, bsz, N_MEM, H_C, HD_C),
            st(ka_s, dec, 1, KVH_A, HEAD_DIM), st(va_s, dec, 1, KVH_A, HEAD_DIM), st(ik_s, dec, 1, D_IDX),
            st(kb_s, dec, 1, KVH_B, HEAD_DIM), st(vb_s, dec, 1, KVH_B, HEAD_DIM), st(logf_s, dec, 1, H_B))
```

```python
import functools

import numpy as np
import jax
import jax.numpy as jnp
from jax import lax
from jax.experimental import pallas as pl
from jax.experimental.pallas import tpu as pltpu

D_MODEL = 4096
BATCH = 4
SEQ = 2048
DEC_BATCH = 32
PAST_LEN = 8192
PAGE_SIZE = 128
HEAD_DIM = 128
H_A = 12
KVH_A = 4
H_IDX = 16
D_IDX = 64
TOPK_MAX = 256
H_B = 12
KVH_B = 4
H_C = 4
HD_C = 256
N_MEM = 256
ROPE_THETA = 500000.0
ROT_FRAC = 4
EPS = 1e-6

F32 = jnp.float32
BF16 = jnp.bfloat16
NEG = -1e30
LANES = 128
HEAD_ROWS = 16
MIB = 1024 * 1024

W_A = H_A * HEAD_DIM
W_B = H_B * HEAD_DIM
W_C = H_C * HD_C
W_KV = KVH_A * HEAD_DIM
COL_TILE = 512
LF_LANE = D_IDX + H_IDX


def _layout():
    src, acc = {}, 0
    for n, s in zip(("qa", "ka", "va", "iq", "ik", "iw", "za", "qb", "kb", "vb", "fb", "zb", "qc", "zc", "gates"),
                    (W_A, W_KV, W_KV, H_IDX * D_IDX, D_IDX, H_IDX, W_A, W_B, W_KV, W_KV, H_B, W_B, W_C, W_C,
                     3 * D_MODEL)):
        src[n] = acc
        acc += s
    names = ("qa", "qb", "za", "zb", "ka", "va", "kb", "vb", "iq", "qc", "zc", "gates", "s1", "s2")
    sizes = (W_A, W_B, W_A, W_B, W_KV, W_KV, W_KV, W_KV, H_IDX * D_IDX, W_C, W_C, 3 * D_MODEL, COL_TILE, COL_TILE)
    starts = dict(src, s1=src["ik"], s2=src["fb"])
    off, tiles, acc = {}, [], 0
    for n, s in zip(names, sizes):
        off[n] = acc
        acc += s
        tiles += [starts[n] + COL_TILE * i for i in range(s // COL_TILE)]
    return off, acc, tiles


def _params(n_axes, vmem_mib):
    return pltpu.CompilerParams(dimension_semantics=("arbitrary",) * n_axes,
                                vmem_limit_bytes=vmem_mib * MIB)


def _sigmoid(z):
    return 1.0 / (1.0 + jnp.exp(-z))


def _silu(z):
    return z * _sigmoid(z)


def _log_sigmoid(z):
    return -(jnp.maximum(-z, 0.0) + jnp.log1p(jnp.exp(-jnp.abs(z))))


def _rot(x, c, sa, sb, half):
    n = x.shape[-1]
    return x * c + pltpu.roll(x, n - half, 1) * sa + pltpu.roll(x, half, 1) * sb


def _dot_nt(a, b):
    return lax.dot_general(a, b, (((1,), (1,)), ((), ())), preferred_element_type=F32)


def _split3(x):
    hi = x.astype(BF16)
    r = x - hi.astype(F32)
    mid = r.astype(BF16)
    lo = (r - mid.astype(F32)).astype(BF16)
    return hi, mid, lo


KEY_NEG_INF = -2139095041
KEY_POS_INF = 2139095040


def _key_to_float(key):
    key = jnp.clip(key, KEY_NEG_INF, KEY_POS_INF)
    return lax.bitcast_convert_type(jnp.where(key < 0, key ^ jnp.int32(0x7FFFFFFF), key), F32)


def _kth_largest(count_ge, rows, k):
    kf = jnp.float32(k)
    zero = jnp.zeros((rows, 1), jnp.int32)
    thr = jnp.where(count_ge(_key_to_float(zero)) >= kf, zero, jnp.int32(-2 ** 31))

    def body(i, thr):
        cand = thr + jnp.left_shift(jnp.int32(1), jnp.int32(30) - i)
        return jnp.where(count_ge(_key_to_float(cand)) >= kf, cand, thr)

    return _key_to_float(lax.fori_loop(0, 31, body, thr))


def _rmsnorm_body(x_ref, g_ref, o_ref):
    x = x_ref[...]
    ms = jnp.mean(x * x, axis=-1, keepdims=True)
    o_ref[...] = ((x * lax.rsqrt(ms + EPS)) * g_ref[...]).astype(o_ref.dtype)


def _rmsnorm(x, g, out_dtype, bm):
    m, d = x.shape
    bm = min(bm, m)
    return pl.pallas_call(
        _rmsnorm_body,
        grid=(m // bm,),
        in_specs=[pl.BlockSpec((bm, d), lambda i: (i, 0)), pl.BlockSpec((1, d), lambda i: (0, 0))],
        out_specs=pl.BlockSpec((bm, d), lambda i: (i, 0)),
        out_shape=jax.ShapeDtypeStruct((m, d), out_dtype),
        compiler_params=_params(1, 40),
        name="rmsnorm",
    )(x, g.reshape(1, d))


def _mm_body(x_ref, w_ref, o_ref):
    o_ref[...] = jnp.dot(x_ref[...], w_ref[...], preferred_element_type=F32)


def _mm_nt_body(x_ref, wt_ref, o_ref):
    o_ref[...] = _dot_nt(x_ref[...], wt_ref[...])


def _matmul_nt(x, wt, bm, bn, name):
    m, k = x.shape
    n = wt.shape[0]
    bm, bn = min(bm, m), min(bn, n)
    if n % bn:
        bn = COL_TILE
    assert m % bm == 0 and n % bn == 0
    return pl.pallas_call(
        _mm_nt_body,
        grid=(m // bm, n // bn),
        in_specs=[pl.BlockSpec((bm, k), lambda i, j: (i, 0)), pl.BlockSpec((bn, k), lambda i, j: (j, 0))],
        out_specs=pl.BlockSpec((bm, bn), lambda i, j: (i, j)),
        out_shape=jax.ShapeDtypeStruct((m, n), F32),
        compiler_params=_params(2, 52),
        name=name,
    )(x, wt)


def _matmul(x, w, bm, bn, name):
    m, k = x.shape
    n = w.shape[1]
    bm, bn = min(bm, m), min(bn, n)
    return pl.pallas_call(
        _mm_body,
        grid=(m // bm, n // bn),
        in_specs=[pl.BlockSpec((bm, k), lambda i, j: (i, 0)), pl.BlockSpec((k, bn), lambda i, j: (0, j))],
        out_specs=pl.BlockSpec((bm, bn), lambda i, j: (i, j)),
        out_shape=jax.ShapeDtypeStruct((m, n), F32),
        compiler_params=_params(2, 48),
        name=name,
    )(x, w)


def _wprep_body(shifts, tab_ref, a_ref, b_ref, o_ref):
    shift = tab_ref[1, pl.program_id(1)]

    @pl.when(shift == 0)
    def _():
        o_ref[...] = a_ref[...].astype(o_ref.dtype)

    for sh in shifts:
        @pl.when(shift == sh)
        def _(sh=sh):
            x = jnp.concatenate([a_ref[...], b_ref[...]], axis=0).astype(F32)
            o_ref[...] = pltpu.roll(x, 2 * COL_TILE - sh, 0)[:COL_TILE, :].astype(o_ref.dtype)


def _wprep(wt, tiles, bk):
    n_src, k = wt.shape
    last = pl.cdiv(n_src, COL_TILE) - 1
    tab = jnp.asarray([[c // COL_TILE for c in tiles], [c % COL_TILE for c in tiles]], jnp.int32)
    shifts = tuple(sorted({c % COL_TILE for c in tiles} - {0}))
    bk = min(bk, k)
    grid_spec = pltpu.PrefetchScalarGridSpec(
        num_scalar_prefetch=1,
        grid=(k // bk, len(tiles)),
        in_specs=[pl.BlockSpec((COL_TILE, bk), lambda i, t, tab: (tab[0, t], i)),
                  pl.BlockSpec((COL_TILE, bk), lambda i, t, tab: (jnp.minimum(tab[0, t] + 1, last), i))],
        out_specs=pl.BlockSpec((COL_TILE, bk), lambda i, t, tab: (t, i)),
    )
    return pl.pallas_call(
        functools.partial(_wprep_body, shifts),
        grid_spec=grid_spec,
        out_shape=jax.ShapeDtypeStruct((COL_TILE * len(tiles), k), BF16),
        compiler_params=_params(2, 40),
        name="w_prep",
    )(tab, wt, wt)


def _cast_body(x_ref, o_ref):
    o_ref[...] = x_ref[...].astype(o_ref.dtype)


def _cast_bf16(w, bm):
    m, n = w.shape
    return pl.pallas_call(
        _cast_body,
        grid=(m // bm,),
        in_specs=[pl.BlockSpec((bm, n), lambda i: (i, 0))],
        out_specs=pl.BlockSpec((bm, n), lambda i: (i, 0)),
        out_shape=jax.ShapeDtypeStruct((m, n), BF16),
        compiler_params=_params(1, 40),
        name="w_cast",
    )(w)


def _post_body(ka_ref, s1_ref, s2_ref, tab_ref, bf_ref, ka_o, sm_o):
    c, sa, sb = tab_ref[0], tab_ref[1], tab_ref[2]
    for h in range(KVH_A):
        sl = slice(h * HEAD_DIM, (h + 1) * HEAD_DIM)
        ka_o[:, sl] = _rot(ka_ref[:, sl], c, sa, sb, HEAD_DIM // ROT_FRAC // 2)
    r = _rot(s1_ref[...], tab_ref[6], tab_ref[7], tab_ref[8], D_IDX // ROT_FRAC // 2)
    lf = _log_sigmoid(pltpu.roll(s2_ref[...], LF_LANE, 1) + bf_ref[...])
    lane = lax.broadcasted_iota(jnp.int32, r.shape, 1)
    sm_o[...] = jnp.where(lane < LF_LANE, r, jnp.where(lane < LF_LANE + H_B, lf, 0.0))


def _post(proj, tab, bf_row, seq_rows, bm):
    off = _layout()[0]
    m = proj.shape[0]
    bm = min(bm, m)
    tb = seq_rows // bm
    return pl.pallas_call(
        _post_body,
        grid=(m // bm,),
        in_specs=[pl.BlockSpec((bm, W_KV), lambda i: (i, off["ka"] // W_KV)),
                  pl.BlockSpec((bm, LANES), lambda i: (i, off["s1"] // LANES)),
                  pl.BlockSpec((bm, LANES), lambda i: (i, off["s2"] // LANES)),
                  pl.BlockSpec((9, bm, LANES), lambda i: (0, i % tb, 0)),
                  pl.BlockSpec((1, LANES), lambda i: (0, 0))],
        out_specs=[pl.BlockSpec((bm, W_KV), lambda i: (i, 0)), pl.BlockSpec((bm, LANES), lambda i: (i, 0))],
        out_shape=[jax.ShapeDtypeStruct((m, W_KV), F32), jax.ShapeDtypeStruct((m, LANES), F32)],
        compiler_params=_params(1, 40),
        name="kv_post",
    )(proj, proj, proj, tab, bf_row)


def _qrot_body(qa_ref, iq_ref, tab_ref, qa_o, iq_o):
    for h in range(H_A):
        sl = slice(h * HEAD_DIM, (h + 1) * HEAD_DIM)
        qa_o[:, sl] = _rot(qa_ref[:, sl], tab_ref[0], tab_ref[1], tab_ref[2], HEAD_DIM // ROT_FRAC // 2)
    for hp in range(H_IDX * D_IDX // LANES):
        sl = slice(hp * LANES, (hp + 1) * LANES)
        iq_o[:, sl] = _rot(iq_ref[:, sl], tab_ref[3], tab_ref[4], tab_ref[5], D_IDX // ROT_FRAC // 2)


def _qrot(proj, tab):
    off = _layout()[0]
    m = proj.shape[0]
    wi = H_IDX * D_IDX
    return pl.pallas_call(
        _qrot_body,
        grid=(1,),
        in_specs=[pl.BlockSpec((m, W_A), lambda i: (0, off["qa"] // W_A)),
                  pl.BlockSpec((m, wi), lambda i: (0, off["iq"] // wi)),
                  pl.BlockSpec((9, m, LANES), lambda i: (0, 0, 0))],
        out_specs=[pl.BlockSpec((m, W_A), lambda i: (0, 0)), pl.BlockSpec((m, wi), lambda i: (0, 0))],
        out_shape=[jax.ShapeDtypeStruct((m, W_A), F32), jax.ShapeDtypeStruct((m, wi), F32)],
        compiler_params=_params(1, 32),
        name="q_rotary",
    )(proj, proj, tab)


def _cumsum_body(sm_ref, ccol_ref, crow_ref):
    r = lax.broadcasted_iota(jnp.int32, (LANES, LANES), 0)
    c = lax.broadcasted_iota(jnp.int32, (LANES, LANES), 1)
    tri = (c <= r).astype(BF16)
    keep = (c >= LF_LANE) & (c < LF_LANE + H_B)
    carry = jnp.zeros((1, LANES), F32)
    for blk in range(ccol_ref.shape[0] // LANES):
        rows = slice(blk * LANES, (blk + 1) * LANES)
        x = jnp.where(keep, sm_ref[rows, :], 0.0)
        hi, mid, lo = _split3(x)
        cb = (jnp.dot(tri, hi, preferred_element_type=F32) + jnp.dot(tri, mid, preferred_element_type=F32)
              + jnp.dot(tri, lo, preferred_element_type=F32)) + carry
        carry = cb[LANES - 1:LANES, :]
        ccol_ref[rows, :] = cb
        per = crow_ref.shape[3] // LANES
        crow_ref[0, blk // per, :, (blk % per) * LANES:(blk % per + 1) * LANES] = cb.T


def _cumsum(small, batch, seq, tk):
    return pl.pallas_call(
        _cumsum_body,
        grid=(batch,),
        in_specs=[pl.BlockSpec((seq, LANES), lambda b: (b, 0))],
        out_specs=[pl.BlockSpec((seq, LANES), lambda b: (b, 0)),
                   pl.BlockSpec((1, seq // tk, LANES, tk), lambda b: (b, 0, 0, 0))],
        out_shape=[jax.ShapeDtypeStruct((batch * seq, LANES), F32),
                   jax.ShapeDtypeStruct((batch, seq // tk, LANES, tk), F32)],
        compiler_params=_params(1, 32),
        name="fox_cumsum",
    )(small)


def _softmax_av(s, v):
    m = jnp.max(s, axis=1, keepdims=True)
    p = jnp.exp(s - m)
    l = jnp.sum(p, axis=1, keepdims=True)
    return jnp.dot(p.astype(BF16), v, preferred_element_type=F32) / l


def _lane_pieces(x):
    return [x[:, t * LANES:(t + 1) * LANES] for t in range(x.shape[1] // LANES)]


def _lane_tile(x, width):
    return jnp.concatenate([x] * (width // LANES), axis=1)


def _attend_two_pass(n_heads, hd, qi, logits, values, last_mask, z_ref, u_ref, mx_ref, ls_ref, acc_ref):
    tq = tk = mx_ref.shape[1]

    def each_chunk(step):
        if last_mask is None:
            lax.fori_loop(0, qi + 1, lambda ci, c: (step(ci, None), c)[1], 0)
        else:
            lax.fori_loop(0, qi, lambda ci, c: (step(ci, None), c)[1], 0)
            step(qi, last_mask)

    def masked(ci, h, mask):
        s = logits(ci, h)
        return s if mask is None else s + mask

    mx_ref[...] = jnp.full(mx_ref.shape, NEG, F32)

    def row_max(ci, mask):
        for h in range(n_heads):
            m = mx_ref[h]
            for piece in _lane_pieces(masked(ci, h, mask)):
                m = jnp.maximum(m, piece)
            mx_ref[h] = m

    each_chunk(row_max)
    for h in range(n_heads):
        mx_ref[h] = jnp.broadcast_to(jnp.max(mx_ref[h], axis=1, keepdims=True), (tq, LANES))
    ls_ref[...] = jnp.zeros(ls_ref.shape, F32)
    acc_ref[...] = jnp.zeros(acc_ref.shape, F32)

    def accumulate(ci, mask):
        for h in range(n_heads):
            sl = slice(h * hd, (h + 1) * hd)
            p = jnp.exp(masked(ci, h, mask) - _lane_tile(mx_ref[h], tk))
            l = ls_ref[h]
            for piece in _lane_pieces(p):
                l = l + piece
            ls_ref[h] = l
            acc_ref[:, sl] += jnp.dot(p.astype(BF16), values(ci, h), preferred_element_type=F32)

    each_chunk(accumulate)
    for h in range(n_heads):
        sl = slice(h * hd, (h + 1) * hd)
        l = jnp.sum(ls_ref[h], axis=1, keepdims=True)
        u_ref[:, sl] = ((acc_ref[:, sl] / l) * _silu(z_ref[:, sl])).astype(u_ref.dtype)


def _chunk_rows(c, tk):
    return pl.ds(pl.multiple_of(c * tk, tk), tk)


def _dsa_prompt_body(topk, q_ref, iq_ref, sm_ref, z_ref, k_ref, v_ref, ik_ref, tab_ref, u_ref,
                     xb_ref, w_ref, sc_ref, bias_ref, qb_ref, mx_ref, ls_ref, acc_ref):
    tq = q_ref.shape[0]
    qi = pl.program_id(1)
    r = lax.broadcasted_iota(jnp.int32, (tq, tq), 0)
    c = lax.broadcasted_iota(jnp.int32, (tq, tq), 1)
    tri = c <= r

    for hp in range(H_IDX * D_IDX // LANES):
        sl = slice(hp * LANES, (hp + 1) * LANES)
        xb_ref[:, sl] = _rot(iq_ref[:, sl], tab_ref[3], tab_ref[4], tab_ref[5], D_IDX // ROT_FRAC // 2).astype(BF16)
    for h in range(H_IDX):
        w_ref[h] = jnp.broadcast_to(sm_ref[:, D_IDX + h:D_IDX + h + 1] * (H_IDX ** -0.5), (tq, LANES))

    def scores(ci):
        rows = _chunk_rows(ci, tq)
        acc = jnp.zeros((tq, tq), F32)
        for hp in range(H_IDX * D_IDX // LANES):
            xb = xb_ref[:, hp * LANES:(hp + 1) * LANES]
            for j in range(LANES // D_IDX):
                h = hp * (LANES // D_IDX) + j
                s = _dot_nt(xb, ik_ref[rows, j * LANES:(j + 1) * LANES]) * (D_IDX ** -0.5)
                acc = acc + jnp.maximum(s, 0.0) * _lane_tile(w_ref[h], tq)
        return acc

    def fill_scores(ci, carry):
        sc_ref[ci] = scores(ci)
        return carry

    lax.fori_loop(0, qi, fill_scores, 0)
    sc_ref[qi] = jnp.where(tri, scores(qi), -jnp.inf)

    def count(hit):
        def add(ci, cnt):
            for piece in _lane_pieces(jnp.where(hit(sc_ref[ci]), 1.0, 0.0)):
                cnt = cnt + piece
            return cnt
        cnt = lax.fori_loop(0, qi + 1, add, jnp.zeros((tq, LANES), F32))
        return jnp.sum(cnt, axis=1, keepdims=True)

    thr = _kth_largest(lambda t: count(lambda x: x >= t), tq, topk)

    need = jnp.float32(topk) - count(lambda x: x > thr)
    prefix = (r <= c).astype(BF16)

    def keep_bias(ci, run, mask):
        kk = sc_ref[ci]
        tie = kk == thr
        tie_f = jnp.where(tie, 1.0, 0.0)
        seen = jnp.dot(tie_f.astype(BF16), prefix, preferred_element_type=F32) + run
        keep = (kk > thr) | (tie & (seen <= need))
        if mask is not None:
            keep = keep & mask
        bias_ref[ci] = jnp.where(keep, 0.0, NEG)
        return run + jnp.sum(tie_f, axis=1, keepdims=True)

    run = lax.fori_loop(0, qi, lambda ci, run: keep_bias(ci, run, None), jnp.zeros((tq, 1), F32))
    keep_bias(qi, run, tri)

    grp = H_A // KVH_A
    for h in range(H_A):
        sl = slice(h * HEAD_DIM, (h + 1) * HEAD_DIM)
        qb_ref[:, sl] = _rot(q_ref[:, sl], tab_ref[0], tab_ref[1], tab_ref[2],
                             HEAD_DIM // ROT_FRAC // 2).astype(BF16)

    def kv_cols(h):
        return slice(h // grp * HEAD_DIM, (h // grp + 1) * HEAD_DIM)

    def logits(ci, h):
        qh = qb_ref[:, h * HEAD_DIM:(h + 1) * HEAD_DIM]
        return _dot_nt(qh, k_ref[_chunk_rows(ci, tq), kv_cols(h)]) * (HEAD_DIM ** -0.5) + bias_ref[ci]

    def values(ci, h):
        return v_ref[_chunk_rows(ci, tq), kv_cols(h)]

    _attend_two_pass(H_A, HEAD_DIM, qi, logits, values, None, z_ref, u_ref, mx_ref, ls_ref, acc_ref)


def _dsa_prompt(proj, small, ka_bf, va_bf, ik_bf, tab, batch, seq, tq):
    off = _layout()[0]
    nq = seq // tq
    wi = H_IDX * D_IDX
    topk = min(TOPK_MAX, seq // 4)
    row = lambda b, q: b * nq + q
    return pl.pallas_call(
        functools.partial(_dsa_prompt_body, topk),
        grid=(batch, nq),
        scratch_shapes=[pltpu.VMEM((tq, wi), BF16), pltpu.VMEM((H_IDX, tq, LANES), F32),
                        pltpu.VMEM((nq, tq, tq), F32), pltpu.VMEM((nq, tq, tq), F32),
                        pltpu.VMEM((tq, W_A), BF16), pltpu.VMEM((H_A, tq, LANES), F32),
                        pltpu.VMEM((H_A, tq, LANES), F32), pltpu.VMEM((tq, W_A), F32)],
        in_specs=[pl.BlockSpec((tq, W_A), lambda b, q: (row(b, q), off["qa"] // W_A)),
                  pl.BlockSpec((tq, wi), lambda b, q: (row(b, q), off["iq"] // wi)),
                  pl.BlockSpec((tq, LANES), lambda b, q: (row(b, q), 0)),
                  pl.BlockSpec((tq, W_A), lambda b, q: (row(b, q), off["za"] // W_A)),
                  pl.BlockSpec((seq, W_KV), lambda b, q: (b, 0)),
                  pl.BlockSpec((seq, W_KV), lambda b, q: (b, 0)),
                  pl.BlockSpec((seq, 2 * LANES), lambda b, q: (b, 0)),
                  pl.BlockSpec((9, tq, LANES), lambda b, q: (0, q, 0))],
        out_specs=pl.BlockSpec((tq, W_A), lambda b, q: (row(b, q), 0)),
        out_shape=jax.ShapeDtypeStruct((batch * seq, W_A), BF16),
        compiler_params=_params(2, 56),
        name="dsa_prompt",
    )(proj, proj, small, proj, ka_bf, va_bf, ik_bf, tab)


def _fox_prompt_body(q_ref, z_ref, k_ref, v_ref, ccol_ref, crow_ref, u_ref, qb_ref, cq_ref, mx_ref, ls_ref, acc_ref):
    tq = q_ref.shape[0]
    qi = pl.program_id(1)
    r = lax.broadcasted_iota(jnp.int32, (tq, tq), 0)
    c = lax.broadcasted_iota(jnp.int32, (tq, tq), 1)
    diag_bias = jnp.where(c <= r, 0.0, NEG)
    grp = H_B // KVH_B
    qb_ref[...] = q_ref[...].astype(BF16)
    for h in range(H_B):
        cq_ref[h] = jnp.broadcast_to(ccol_ref[:, LF_LANE + h:LF_LANE + h + 1], (tq, LANES))

    def kv_cols(h):
        return slice(h // grp * HEAD_DIM, (h // grp + 1) * HEAD_DIM)

    def logits(ci, h):
        qh = qb_ref[:, h * HEAD_DIM:(h + 1) * HEAD_DIM]
        s = _dot_nt(qh, k_ref[_chunk_rows(ci, tq), kv_cols(h)]) * (HEAD_DIM ** -0.5)
        return s + _lane_tile(cq_ref[h], tq) - crow_ref[0, ci, LF_LANE + h:LF_LANE + h + 1, :]

    def values(ci, h):
        return v_ref[_chunk_rows(ci, tq), kv_cols(h)]

    _attend_two_pass(H_B, HEAD_DIM, qi, logits, values, diag_bias, z_ref, u_ref, mx_ref, ls_ref, acc_ref)


def _fox_prompt(proj, kb_bf, vb_bf, ccol, crow, batch, seq, tq):
    off = _layout()[0]
    nq = seq // tq
    row = lambda b, q: b * nq + q
    return pl.pallas_call(
        _fox_prompt_body,
        grid=(batch, nq),
        in_specs=[pl.BlockSpec((tq, W_B), lambda b, q: (row(b, q), off["qb"] // W_B)),
                  pl.BlockSpec((tq, W_B), lambda b, q: (row(b, q), off["zb"] // W_B)),
                  pl.BlockSpec((seq, W_KV), lambda b, q: (b, 0)),
                  pl.BlockSpec((seq, W_KV), lambda b, q: (b, 0)),
                  pl.BlockSpec((tq, LANES), lambda b, q: (row(b, q), 0)),
                  pl.BlockSpec((1, nq, LANES, tq), lambda b, q: (b, 0, 0, 0))],
        out_specs=pl.BlockSpec((tq, W_B), lambda b, q: (row(b, q), 0)),
        scratch_shapes=[pltpu.VMEM((tq, W_B), BF16), pltpu.VMEM((H_B, tq, LANES), F32),
                        pltpu.VMEM((H_B, tq, LANES), F32), pltpu.VMEM((H_B, tq, LANES), F32),
                        pltpu.VMEM((tq, W_B), F32)],
        out_shape=jax.ShapeDtypeStruct((batch * seq, W_B), BF16),
        compiler_params=_params(2, 56),
        name="fox_prompt",
    )(proj, proj, kb_bf, vb_bf, ccol, crow)


def _cross_prompt_body(q_ref, z_ref, k_ref, v_ref, u_ref):
    for h in range(H_C):
        sl = slice(h * HD_C, (h + 1) * HD_C)
        s = _dot_nt(q_ref[:, sl].astype(BF16), k_ref[:, sl]) * (HD_C ** -0.5)
        o = _softmax_av(s, v_ref[:, sl])
        u_ref[:, sl] = (o * _silu(z_ref[:, sl])).astype(u_ref.dtype)


def _cross_prompt(proj, mk_bf, mv_bf, batch, seq, tq):
    off = _layout()[0]
    nq = seq // tq
    row = lambda b, q: b * nq + q
    return pl.pallas_call(
        _cross_prompt_body,
        grid=(batch, nq),
        in_specs=[pl.BlockSpec((tq, W_C), lambda b, q: (row(b, q), off["qc"] // W_C)),
                  pl.BlockSpec((tq, W_C), lambda b, q: (row(b, q), off["zc"] // W_C)),
                  pl.BlockSpec((N_MEM, W_C), lambda b, q: (b, 0)),
                  pl.BlockSpec((N_MEM, W_C), lambda b, q: (b, 0))],
        out_specs=pl.BlockSpec((tq, W_C), lambda b, q: (row(b, q), 0)),
        out_shape=jax.ShapeDtypeStruct((batch * seq, W_C), BF16),
        compiler_params=_params(2, 40),
        name="cross_prompt",
    )(proj, proj, mk_bf, mv_bf)


def _merge_body(ua_ref, ub_ref, uc_ref, wa_ref, wb_ref, wc_ref, ga_ref, gb_ref, gc_ref, h_ref):
    h = (_sigmoid(ga_ref[...]) * jnp.dot(ua_ref[...], wa_ref[...], preferred_element_type=F32)
         + _sigmoid(gb_ref[...]) * jnp.dot(ub_ref[...], wb_ref[...], preferred_element_type=F32)
         + _sigmoid(gc_ref[...]) * jnp.dot(uc_ref[...], wc_ref[...], preferred_element_type=F32))
    h_ref[...] = h.astype(h_ref.dtype)


def _merge(ua, ub, uc, wa, wb, wc, proj, bm, bn):
    off = _layout()[0]
    m = ua.shape[0]
    d = wa.shape[1]
    bm, bn = min(bm, m), min(bn, d)
    g0 = off["gates"] // bn
    nd = d // bn
    return pl.pallas_call(
        _merge_body,
        grid=(m // bm, nd),
        in_specs=[pl.BlockSpec((bm, W_A), lambda i, j: (i, 0)),
                  pl.BlockSpec((bm, W_B), lambda i, j: (i, 0)),
                  pl.BlockSpec((bm, W_C), lambda i, j: (i, 0)),
                  pl.BlockSpec((W_A, bn), lambda i, j: (0, j)),
                  pl.BlockSpec((W_B, bn), lambda i, j: (0, j)),
                  pl.BlockSpec((W_C, bn), lambda i, j: (0, j)),
                  pl.BlockSpec((bm, bn), lambda i, j: (i, g0 + j)),
                  pl.BlockSpec((bm, bn), lambda i, j: (i, g0 + nd + j)),
                  pl.BlockSpec((bm, bn), lambda i, j: (i, g0 + 2 * nd + j))],
        out_specs=pl.BlockSpec((bm, bn), lambda i, j: (i, j)),
        out_shape=jax.ShapeDtypeStruct((m, d), BF16),
        compiler_params=_params(2, 48),
        name="merge",
    )(ua, ub, uc, wa, wb, wc, proj, proj, proj)


def _outproj_body(x_ref, h_ref, w_ref, g_ref, o_ref, y_ref):
    j = pl.program_id(1)
    nj = pl.num_programs(1)
    bn = w_ref.shape[1]
    y_ref[j] = x_ref[...] + jnp.dot(h_ref[...], w_ref[...], preferred_element_type=F32)

    @pl.when(j == nj - 1)
    def _():
        n_tiles = y_ref.shape[0]
        ss = jnp.zeros((y_ref.shape[1], 1), F32)
        for t in range(n_tiles):
            y = y_ref[t]
            ss = ss + jnp.sum(y * y, axis=-1, keepdims=True)
        inv = lax.rsqrt(ss * (1.0 / (n_tiles * bn)) + EPS)
        for t in range(n_tiles):
            o_ref[:, t * bn:(t + 1) * bn] = (y_ref[t] * inv) * g_ref[:, t * bn:(t + 1) * bn]


def _outproj_norm(x, h, w, g, bm, bn):
    m, d = x.shape
    bm, bn = min(bm, m), min(bn, d)
    return pl.pallas_call(
        _outproj_body,
        grid=(m // bm, d // bn),
        in_specs=[pl.BlockSpec((bm, bn), lambda i, j: (i, j)),
                  pl.BlockSpec((bm, d), lambda i, j: (i, 0)),
                  pl.BlockSpec((d, bn), lambda i, j: (0, j)),
                  pl.BlockSpec((1, d), lambda i, j: (0, 0))],
        out_specs=pl.BlockSpec((bm, d), lambda i, j: (i, 0)),
        out_shape=jax.ShapeDtypeStruct((m, d), F32),
        scratch_shapes=[pltpu.VMEM((d // bn, bm, bn), F32)],
        compiler_params=_params(2, 52),
        name="out_proj",
    )(x, h, w, g.reshape(1, d))


def _idx_sample_body(pps, pt_ref, iq_ref, w_ref, ikn_ref, *rest):
    pages = rest[:pps]
    sc_ref, own_ref = rest[pps], rest[pps + 1]
    iq = iq_ref[0]
    w = w_ref[0] * (H_IDX ** -0.5)
    for i in range(pps):
        s = jnp.dot(iq, pages[i][...].astype(BF16), preferred_element_type=F32) * (D_IDX ** -0.5)
        sc_ref[0, :, i * PAGE_SIZE:(i + 1) * PAGE_SIZE] = jnp.sum(jnp.maximum(s, 0.0) * w, axis=0, keepdims=True)

    @pl.when(pl.program_id(1) == 0)
    def _():
        kn = ikn_ref[0].astype(F32)
        s = jnp.sum(iq.astype(F32) * kn, axis=1, keepdims=True) * (D_IDX ** -0.5)
        own = jnp.sum(jnp.maximum(s, 0.0) * w, axis=0, keepdims=True)
        lane = lax.broadcasted_iota(jnp.int32, (1, LANES), 1)
        own_ref[0] = jnp.where(lane == 0, own, -jnp.inf)


def _idx_sample(page_table, iq_bf, iw, ik_new_bf, pool_idx, pps):
    b, n_pages = page_table.shape
    npc = n_pages // pps
    page_specs = [pl.BlockSpec((None, None, D_IDX, PAGE_SIZE),
                               lambda bi, pc, pt, i=i: (0, pt[bi, pc * pps + i], 0, 0)) for i in range(pps)]
    grid_spec = pltpu.PrefetchScalarGridSpec(
        num_scalar_prefetch=1,
        grid=(b, npc),
        in_specs=[pl.BlockSpec((1, H_IDX, D_IDX), lambda bi, pc, pt: (bi, 0, 0)),
                  pl.BlockSpec((1, H_IDX, 1), lambda bi, pc, pt: (bi, 0, 0)),
                  pl.BlockSpec((1, 1, D_IDX), lambda bi, pc, pt: (bi, 0, 0))] + page_specs,
        out_specs=[pl.BlockSpec((1, 1, pps * PAGE_SIZE), lambda bi, pc, pt: (bi, 0, pc)),
                   pl.BlockSpec((1, 1, LANES), lambda bi, pc, pt: (bi, 0, 0))],
    )
    return pl.pallas_call(
        functools.partial(_idx_sample_body, pps),
        grid_spec=grid_spec,
        out_shape=[jax.ShapeDtypeStruct((b, 1, n_pages * PAGE_SIZE), F32),
                   jax.ShapeDtypeStruct((b, 1, LANES), F32)],
        compiler_params=_params(2, 32),
        name="idx_sample",
    )(page_table, iq_bf, iw, ik_new_bf, *([pool_idx] * pps))


def _topk_sample_body(topk, n_valid, sc_ref, pos_ref, rank_ref, sq_ref, acc_ref):
    sc = sc_ref[...]
    nb = sc.shape[0]
    lane = lax.broadcasted_iota(jnp.int32, sc.shape, 1)
    valid = lane < n_valid
    key = jnp.where(valid, sc, -jnp.inf)

    def count(hit):
        return jnp.sum(jnp.where(hit, 1.0, 0.0), axis=1, keepdims=True)

    thr = _kth_largest(lambda t: count(key >= t), nb, topk)
    need = jnp.float32(topk) - count(key > thr)
    pr = lax.broadcasted_iota(jnp.int32, (LANES, LANES), 0)
    pc = lax.broadcasted_iota(jnp.int32, (LANES, LANES), 1)
    prefix = (pr <= pc).astype(BF16)
    ties = jnp.zeros((nb, 1), F32)
    kept = jnp.zeros((nb, 1), F32)
    n_chunks = sc.shape[1] // LANES
    for c in range(n_chunks):
        kk = key[:, c * LANES:(c + 1) * LANES]
        tie = kk == thr
        tie_f = jnp.where(tie, 1.0, 0.0)
        seen = jnp.dot(tie_f.astype(BF16), prefix, preferred_element_type=F32) + ties
        in_range = lax.broadcasted_iota(jnp.int32, kk.shape, 1) < n_valid - c * LANES
        keep_f = jnp.where(((kk > thr) | (tie & (seen <= need))) & in_range, 1.0, 0.0)
        order = jnp.dot(keep_f.astype(BF16), prefix, preferred_element_type=F32) + kept
        rank_ref[c] = keep_f * order
        ties = ties + jnp.sum(tie_f, axis=1, keepdims=True)
        kept = kept + jnp.sum(keep_f, axis=1, keepdims=True)

    sq_ref[...] = jnp.zeros(sq_ref.shape, F32)
    acc_ref[...] = jnp.zeros(acc_ref.shape, F32)
    slot_no = (lax.broadcasted_iota(jnp.int32, (LANES, topk), 1) + 1).astype(F32)
    key_no = lax.broadcasted_iota(jnp.int32, (LANES, 1), 0)

    def place(c, carry):
        sq_ref[0:nb, :] = rank_ref[c]
        ranks = sq_ref[...].T
        position = (key_no + c * LANES).astype(F32)
        for b in range(nb):
            hit = ranks[:, b:b + 1] == slot_no
            acc_ref[b:b + 1, :] += jnp.sum(jnp.where(hit, position, 0.0), axis=0, keepdims=True)
        return carry

    lax.fori_loop(0, n_chunks, place, 0)
    pos_ref[...] = acc_ref[...].astype(jnp.int32)


def _topk_sample(scores, n_valid):
    topk = min(TOPK_MAX, n_valid // 4)
    b, n = scores.shape
    return pl.pallas_call(
        functools.partial(_topk_sample_body, topk, n_valid),
        grid=(1,),
        in_specs=[pl.BlockSpec((b, n), lambda i: (0, 0))],
        out_specs=pl.BlockSpec((b, topk), lambda i: (0, 0)),
        out_shape=jax.ShapeDtypeStruct((b, topk), jnp.int32),
        scratch_shapes=[pltpu.VMEM((n // LANES, b, LANES), F32), pltpu.VMEM((LANES, LANES), F32),
                        pltpu.VMEM((b, topk), F32)],
        compiler_params=_params(1, 40),
        name="topk_sample",
    )(scores)


def _dsa_gather_body(scale, grp, n_keys, pos_ref, pt_ref, q_ref, sb_ref, gm_ref, kn_ref, vn_ref, nb_ref, z_ref,
                     k_hbm, v_hbm, u_ref, kbuf, vbuf, sem):
    b = pl.program_id(0)
    n_seq = pl.num_programs(0)
    topk = pos_ref.shape[1]
    slot = b % 2

    def row_copies(seq, j, to_slot):
        p = jnp.minimum(pos_ref[seq, j], n_keys - 1)
        page = pt_ref[seq, lax.shift_right_logical(p, 7)]
        src = pl.ds((p & (PAGE_SIZE - 1)) * grp, grp)
        dst = pl.ds(j * grp, grp)
        return (pltpu.make_async_copy(k_hbm.at[page, src, :], kbuf.at[to_slot, dst, :], sem.at[to_slot]),
                pltpu.make_async_copy(v_hbm.at[page, src, :], vbuf.at[to_slot, dst, :], sem.at[to_slot]))

    def fetch(seq, to_slot):
        def start(j, carry):
            for cp in row_copies(seq, j, to_slot):
                cp.start()
            return carry
        lax.fori_loop(0, topk, start, 0)

    @pl.when(b == 0)
    def _():
        fetch(0, 0)

    @pl.when(b + 1 < n_seq)
    def _():
        fetch(b + 1, 1 - slot)

    def finish(j, carry):
        for cp in row_copies(b, j, slot):
            cp.wait()
        return carry

    lax.fori_loop(0, topk, finish, 0)

    q = q_ref[0]
    s = (_dot_nt(q, kbuf[slot].astype(BF16)) * scale + sb_ref[0]) + gm_ref[...]
    kn = kn_ref[0].astype(BF16).astype(F32)
    vn = vn_ref[0].astype(BF16).astype(F32)
    nb = nb_ref[0]
    on = nb > 0.5 * NEG
    s0 = jnp.where(on, jnp.sum(q.astype(F32) * kn, axis=1, keepdims=True) * scale + nb, NEG)
    m = jnp.maximum(jnp.max(s, axis=1, keepdims=True), s0)
    p = jnp.where(s > 0.5 * NEG, jnp.exp(s - m), 0.0)
    p0 = jnp.where(on, jnp.exp(s0 - m), 0.0)
    l = jnp.sum(p, axis=1, keepdims=True) + p0
    acc = jnp.dot(p.astype(BF16), vbuf[slot].astype(BF16), preferred_element_type=F32) + p0 * vn
    u_ref[0] = (acc / l) * _silu(z_ref[0])


def _dsa_gather(pos, page_table, q, slot_bias, gmask, k_new, v_new, new_bias, z, pool_k, pool_v, scale, grp):
    b, topk = pos.shape
    hd = pool_k.shape[2]
    rows = topk * grp
    n_keys = page_table.shape[1] * PAGE_SIZE
    per_b = lambda bi, pos, pt: (bi, 0, 0)
    head_blk = pl.BlockSpec((1, HEAD_ROWS, hd), per_b)
    grid_spec = pltpu.PrefetchScalarGridSpec(
        num_scalar_prefetch=2,
        grid=(b,),
        in_specs=[head_blk,
                  pl.BlockSpec((1, 1, rows), per_b),
                  pl.BlockSpec((HEAD_ROWS, rows), lambda bi, pos, pt: (0, 0)),
                  head_blk, head_blk,
                  pl.BlockSpec((1, HEAD_ROWS, 1), per_b),
                  head_blk,
                  pl.BlockSpec(memory_space=pl.ANY),
                  pl.BlockSpec(memory_space=pl.ANY)],
        out_specs=head_blk,
        scratch_shapes=[pltpu.VMEM((2, rows, hd), F32), pltpu.VMEM((2, rows, hd), F32),
                        pltpu.SemaphoreType.DMA((2,))],
    )
    return pl.pallas_call(
        functools.partial(_dsa_gather_body, scale, grp, n_keys),
        grid_spec=grid_spec,
        out_shape=jax.ShapeDtypeStruct((b, HEAD_ROWS, hd), F32),
        compiler_params=_params(1, 32),
        name="dsa_sample",
    )(pos, page_table, q, slot_bias, gmask, k_new, v_new, new_bias, z, pool_k, pool_v)


def _fox_bias_body(pps, grp, pt_ref, lfn_ref, *rest):
    pages = rest[:pps]
    d_ref, x_ref, carry_ref = rest[pps], rest[pps + 1], rest[pps + 2]
    rows = PAGE_SIZE * grp
    j = lax.broadcasted_iota(jnp.int32, (LANES, rows), 0)
    r = lax.broadcasted_iota(jnp.int32, (LANES, rows), 1)
    upper = (r < j * grp).astype(BF16)

    @pl.when(pl.program_id(1) == 0)
    def _():
        x_ref[...] = jnp.zeros_like(x_ref)
        x_ref[0, 0:1, :] = lfn_ref[0]
        carry_ref[...] = x_ref[0].T[0:HEAD_ROWS, 0:1]
        x_ref[0, 0:1, :] = jnp.zeros((1, LANES), F32)

    carry = carry_ref[...]
    for i in reversed(range(pps)):
        x_ref[i, :, 0:H_B] = pages[i][...]
        xt = x_ref[i].T[0:HEAD_ROWS, :]
        hi, mid, lo = _split3(xt)
        d_loc = (jnp.dot(hi, upper, preferred_element_type=F32) + jnp.dot(mid, upper, preferred_element_type=F32)
                 + jnp.dot(lo, upper, preferred_element_type=F32))
        d_ref[0, :, i * rows:(i + 1) * rows] = d_loc + carry
        carry = carry + jnp.sum(xt, axis=1, keepdims=True)
    carry_ref[...] = carry


def _fox_bias(page_table, lf_new, pool_lf, pps, grp):
    b, n_pages = page_table.shape
    npc = n_pages // pps
    rows = PAGE_SIZE * grp
    page_specs = [pl.BlockSpec((None, None, PAGE_SIZE, H_B),
                               lambda bi, pc, pt, i=i: (0, pt[bi, (npc - 1 - pc) * pps + i], 0, 0))
                  for i in range(pps)]
    grid_spec = pltpu.PrefetchScalarGridSpec(
        num_scalar_prefetch=1,
        grid=(b, npc),
        in_specs=[pl.BlockSpec((1, 1, LANES), lambda bi, pc, pt: (bi, 0, 0))] + page_specs,
        out_specs=pl.BlockSpec((1, HEAD_ROWS, pps * rows), lambda bi, pc, pt: (bi, 0, npc - 1 - pc)),
        scratch_shapes=[pltpu.VMEM((pps, LANES, LANES), F32), pltpu.VMEM((HEAD_ROWS, 1), F32)],
    )
    return pl.pallas_call(
        functools.partial(_fox_bias_body, pps, grp),
        grid_spec=grid_spec,
        out_shape=jax.ShapeDtypeStruct((b, HEAD_ROWS, n_pages * rows), F32),
        compiler_params=_params(2, 32),
        name="fox_bias_sample",
    )(page_table, lf_new, *([pool_lf] * pps))


def _paged_body(pps, scale, pt_ref, q_ref, bias_ref, gm_ref, kn_ref, vn_ref, nb_ref, z_ref, *rest):
    kps, vps = rest[:pps], rest[pps:2 * pps]
    u_ref, m_ref, l_ref, acc_ref = rest[2 * pps:2 * pps + 4]
    pc = pl.program_id(1)
    rows = kps[0].shape[0]
    q = q_ref[0]

    @pl.when(pc == 0)
    def _():
        kn = kn_ref[0].astype(BF16).astype(F32)
        vn = vn_ref[0].astype(BF16).astype(F32)
        nb = nb_ref[0]
        on = nb > 0.5 * NEG
        s0 = jnp.sum(q.astype(F32) * kn, axis=1, keepdims=True) * scale + nb
        m_ref[...] = jnp.where(on, s0, NEG)
        l_ref[...] = jnp.where(on, 1.0, 0.0)
        acc_ref[...] = jnp.where(on, vn, 0.0)

    ss = [(_dot_nt(q, kps[i][...].astype(BF16)) * scale + bias_ref[0, :, i * rows:(i + 1) * rows]) + gm_ref[...]
          for i in range(pps)]
    m_old = m_ref[...]
    m_new = m_old
    for s in ss:
        m_new = jnp.maximum(m_new, jnp.max(s, axis=1, keepdims=True))
    alpha = jnp.exp(m_old - m_new)
    l_new = alpha * l_ref[...]
    acc = alpha * acc_ref[...]
    for i, s in enumerate(ss):
        pf = jnp.where(s > 0.5 * NEG, jnp.exp(s - m_new), 0.0)
        l_new = l_new + jnp.sum(pf, axis=1, keepdims=True)
        acc = acc + jnp.dot(pf.astype(BF16), vps[i][...].astype(BF16), preferred_element_type=F32)
    l_ref[...] = l_new
    acc_ref[...] = acc
    m_ref[...] = m_new

    @pl.when(pc == pl.num_programs(1) - 1)
    def _():
        u_ref[0] = (acc_ref[...] / l_ref[...]) * _silu(z_ref[0])


def _paged_attn(page_idx, q, bias, gmask, k_new, v_new, new_bias, z, pool_k, pool_v, pps, scale, name, paged=True):
    b, n_pages = page_idx.shape
    rows, hd = (pool_k.shape[1] if paged else pool_k.shape[1] // n_pages), pool_k.shape[2]
    rb = bias.shape[1]
    npc = n_pages // pps
    if paged:
        kspecs = [pl.BlockSpec((None, rows, hd), lambda bi, pc, pt, i=i: (pt[bi, pc * pps + i], 0, 0))
                  for i in range(pps)]
    else:
        kspecs = [pl.BlockSpec((None, rows, hd), lambda bi, pc, pt, i=i: (bi, pc * pps + i, 0)) for i in range(pps)]
    per_b = lambda bi, pc, pt: (bi, 0, 0)
    head_blk = pl.BlockSpec((1, HEAD_ROWS, hd), per_b)
    grid_spec = pltpu.PrefetchScalarGridSpec(
        num_scalar_prefetch=1,
        grid=(b, npc),
        in_specs=[head_blk,
                  pl.BlockSpec((1, rb, pps * rows), lambda bi, pc, pt: (bi, 0, pc)),
                  pl.BlockSpec((HEAD_ROWS, rows), lambda bi, pc, pt: (0, 0)),
                  head_blk, head_blk,
                  pl.BlockSpec((1, HEAD_ROWS, 1), per_b),
                  head_blk] + kspecs + kspecs,
        out_specs=head_blk,
        scratch_shapes=[pltpu.VMEM((HEAD_ROWS, 1), F32), pltpu.VMEM((HEAD_ROWS, 1), F32),
                        pltpu.VMEM((HEAD_ROWS, hd), F32)],
    )
    return pl.pallas_call(
        functools.partial(_paged_body, pps, scale),
        grid_spec=grid_spec,
        out_shape=jax.ShapeDtypeStruct((b, HEAD_ROWS, hd), F32),
        compiler_params=_params(2, 48),
        name=name,
    )(page_idx, q, bias, gmask, k_new, v_new, new_bias, z, *([pool_k] * pps), *([pool_v] * pps))


def _rotary_tables(pos):
    def base(width):
        rot = width // ROT_FRAC
        half = rot // 2
        inv = jnp.power(jnp.float32(ROPE_THETA), -jnp.arange(half, dtype=F32) * (2.0 / rot))
        ang = pos.astype(F32)[:, None] * inv[None, :]
        cos, sin = jnp.cos(ang), jnp.sin(ang)
        n = pos.shape[0]
        ones, zeros = jnp.ones((n, width - rot), F32), jnp.zeros((n, width - rot), F32)
        zh = jnp.zeros((n, half), F32)
        return (jnp.concatenate([cos, cos, ones], 1), jnp.concatenate([-sin, zh, zeros], 1),
                jnp.concatenate([zh, sin, zeros], 1))

    c1, a1, b1 = base(HEAD_DIM)
    c2, a2, b2 = base(D_IDX)
    n = pos.shape[0]
    pad1, pad0 = jnp.ones((n, LANES - D_IDX), F32), jnp.zeros((n, LANES - D_IDX), F32)
    return jnp.stack([c1, a1, b1,
                      jnp.concatenate([c2, c2], 1), jnp.concatenate([a2, a2], 1), jnp.concatenate([b2, b2], 1),
                      jnp.concatenate([c2, pad1], 1), jnp.concatenate([a2, pad0], 1), jnp.concatenate([b2, pad0], 1)])


def _head_rows(x, n_heads, hd):
    b = x.shape[0]
    return jnp.pad(x.reshape(b, n_heads, hd), ((0, 0), (0, HEAD_ROWS - n_heads), (0, 0)))


def _group_rows(x, n_heads, n_groups, hd):
    b = x.shape[0]
    gid = np.minimum(np.arange(HEAD_ROWS), n_heads - 1) // (n_heads // n_groups)
    return x.reshape(b, n_groups, hd)[:, gid]


def _group_mask(n_heads, n_groups, n_keys=PAGE_SIZE):
    gid = np.minimum(np.arange(HEAD_ROWS), n_heads - 1) // (n_heads // n_groups)
    r = np.arange(n_keys * n_groups)
    return jnp.asarray(np.where((r[None, :] % n_groups) == gid[:, None], 0.0, NEG), F32)


def _trunk(x2d, proj, ua, ub, uc, w_a, w_b, w_c, w_o, g_final):
    h = _merge(ua, ub, uc, w_a, w_b, w_c, proj, 1024, 512)
    return _outproj_norm(x2d, h, w_o, g_final, 512, 512)


def kernel(x_prompt, x_sample, cache_a_k, cache_a_v, cache_a_idx, cache_b_k, cache_b_v, cache_b_logf, cache_mem_k,
           cache_mem_v, page_table, mem_prompt, g_norm, w_in, b_forget, w_br_a, w_br_b, w_br_c, w_out, g_mem,
           w_mem_kv, g_final):
    off, _, tiles = _layout()
    bsz, seq, d = x_prompt.shape
    dec = x_sample.shape[0]
    tq = min(256, seq)

    w_in_t = _wprep(jnp.transpose(w_in[0].astype(BF16)), tiles, 4096)
    w_a, w_b, w_c, w_o, w_m = (_cast_bf16(w[0], 256) for w in (w_br_a, w_br_b, w_br_c, w_out, w_mem_kv))
    bf_row = jnp.zeros((1, LANES), F32).at[0, LF_LANE:LF_LANE + H_B].set(b_forget[0])

    xp = x_prompt.reshape(bsz * seq, d)
    proj = _matmul_nt(_rmsnorm(xp, g_norm[0], BF16, 256), w_in_t, 1024, 1024, "in_proj")
    tab_p = _rotary_tables(jnp.arange(seq, dtype=jnp.int32))
    ka_rot, small = _post(proj, tab_p, bf_row, seq, 512)
    col = lambda name, width: proj[:, off[name]:off[name] + width]
    va, kb, vb = col("va", W_KV), col("kb", W_KV), col("vb", W_KV)
    ik = small[:, :D_IDX]
    logf = small[:, LF_LANE:LF_LANE + H_B]

    mem_n = _rmsnorm(mem_prompt.reshape(bsz * N_MEM, d), g_mem[0], BF16, 256)
    mkv = _matmul(mem_n, w_m, 1024, 1024, "mem_kv")
    mk, mv = mkv[:, :W_C], mkv[:, W_C:]

    ik_bf = ik.astype(BF16)
    ik_pad = jnp.zeros_like(ik_bf)
    ik2 = jnp.concatenate([ik_bf, ik_pad, ik_pad, ik_bf], axis=1)
    ua = _dsa_prompt(proj, small, ka_rot.astype(BF16), va.astype(BF16), ik2, tab_p, bsz, seq, tq)
    ccol, crow = _cumsum(small, bsz, seq, tq)
    ub = _fox_prompt(proj, kb.astype(BF16), vb.astype(BF16), ccol, crow, bsz, seq, tq)
    uc = _cross_prompt(proj, mk.astype(BF16), mv.astype(BF16), bsz, seq, tq)
    y_prompt = _trunk(xp, proj, ua, ub, uc, w_a, w_b, w_c, w_o, g_final).reshape(bsz, seq, d)

    xs = x_sample.reshape(dec, d)
    proj_s = _matmul_nt(_rmsnorm(xs, g_norm[0], BF16, dec), w_in_t, dec, 1024, "in_proj_sample")
    tab_s = _rotary_tables(jnp.full((dec,), page_table.shape[1] * PAGE_SIZE, jnp.int32))
    ka_s, small_s = _post(proj_s, tab_s, bf_row, dec, dec)
    cols = lambda name, width: proj_s[:, off[name]:off[name] + width]
    va_s, kb_s, vb_s = cols("va", W_KV), cols("kb", W_KV), cols("vb", W_KV)
    ik_s = small_s[:, :D_IDX]
    iw_s = small_s[:, D_IDX:D_IDX + H_IDX]
    logf_s = small_s[:, LF_LANE:LF_LANE + H_B]
    qa_s, iq_s = _qrot(proj_s, tab_s)

    n_pages = page_table.shape[1]
    pps = min(16, n_pages)
    pps_small = min(32, n_pages)
    scores, own = _idx_sample(page_table, iq_s.astype(BF16).reshape(dec, H_IDX, D_IDX), iw_s.reshape(dec, H_IDX, 1),
                              ik_s.astype(BF16).reshape(dec, 1, D_IDX), jnp.transpose(cache_a_idx, (0, 1, 3, 2)),
                              pps_small)
    n_keys = n_pages * PAGE_SIZE
    n_pool = cache_a_k.shape[1]
    flat = lambda pool, g, hd: pool.reshape(n_pool, PAGE_SIZE * g, hd)
    pos = _topk_sample(jnp.concatenate([scores[:, 0], own[:, 0]], axis=1), n_keys + 1)
    is_new = pos == n_keys
    slot_bias = jnp.repeat(jnp.where(is_new, NEG, 0.0).astype(F32), KVH_A, axis=1)[:, None, :]
    new_bias = jnp.where(jnp.any(is_new, axis=1), 0.0, NEG).astype(F32)

    scale = HEAD_DIM ** -0.5
    ua_full = _dsa_gather(
        pos, page_table, _head_rows(qa_s, H_A, HEAD_DIM).astype(BF16), slot_bias,
        _group_mask(H_A, KVH_A, pos.shape[1]), _group_rows(ka_s, H_A, KVH_A, HEAD_DIM),
        _group_rows(va_s, H_A, KVH_A, HEAD_DIM), jnp.broadcast_to(new_bias[:, None, None], (dec, HEAD_ROWS, 1)),
        _head_rows(cols("za", W_A), H_A, HEAD_DIM),
        flat(cache_a_k, KVH_A, HEAD_DIM), flat(cache_a_v, KVH_A, HEAD_DIM), scale, KVH_A)

    lf_new = jnp.pad(logf_s, ((0, 0), (0, LANES - H_B))).reshape(dec, 1, LANES)
    d_bias = _fox_bias(page_table, lf_new, cache_b_logf, min(16, n_pages), KVH_B)
    ub_full = _paged_attn(
        page_table, _head_rows(cols("qb", W_B), H_B, HEAD_DIM).astype(BF16), d_bias,
        _group_mask(H_B, KVH_B), _group_rows(kb_s, H_B, KVH_B, HEAD_DIM), _group_rows(vb_s, H_B, KVH_B, HEAD_DIM),
        jnp.zeros((dec, HEAD_ROWS, 1), F32), _head_rows(cols("zb", W_B), H_B, HEAD_DIM),
        flat(cache_b_k, KVH_B, HEAD_DIM), flat(cache_b_v, KVH_B, HEAD_DIM), pps, scale, "fox_sample")

    mem_pages = N_MEM // PAGE_SIZE
    uc_full = _paged_attn(
        jnp.zeros((dec, mem_pages), jnp.int32), _head_rows(cols("qc", W_C), H_C, HD_C).astype(BF16),
        jnp.zeros((dec, 1, N_MEM * H_C), F32), _group_mask(H_C, H_C),
        jnp.zeros((dec, HEAD_ROWS, HD_C), F32), jnp.zeros((dec, HEAD_ROWS, HD_C), F32),
        jnp.full((dec, HEAD_ROWS, 1), NEG, F32), _head_rows(cols("zc", W_C), H_C, HD_C),
        cache_mem_k.reshape(dec, N_MEM * H_C, HD_C), cache_mem_v.reshape(dec, N_MEM * H_C, HD_C),
        mem_pages, HD_C ** -0.5, "cross_sample", paged=False)

    ua_s = ua_full[:, :H_A].reshape(dec, W_A).astype(BF16)
    ub_s = ub_full[:, :H_B].reshape(dec, W_B).astype(BF16)
    uc_s = uc_full[:, :H_C].reshape(dec, W_C).astype(BF16)
    y_sample = _trunk(xs, proj_s, ua_s, ub_s, uc_s, w_a, w_b, w_c, w_o, g_final).reshape(dec, 1, d)

    st = lambda a, b, t, *tail: a.reshape((1, b, t) + tail)
    return (y_prompt, y_sample,
            st(ka_rot, bsz, seq, KVH_A, HEAD_DIM), st(va, bsz, seq, KVH_A, HEAD_DIM), st(ik, bsz, seq, D_IDX),
            st(kb, bsz, seq, KVH_B, HEAD_DIM), st(vb, bsz, seq, KVH_B, HEAD_DIM), st(logf, bsz, seq, H_B),
            st(mk, bsz, N_MEM, H_C, HD_C), st(mv, bsz, N_MEM, H_C, HD_C),
            st(ka_s, dec, 1, KVH_A, HEAD_DIM), st(va_s, dec, 1, KVH_A, HEAD_DIM), st(ik_s, dec, 1, D_IDX),
            st(kb_s, dec, 1, KVH_B, HEAD_DIM), st(vb_s, dec, 1, KVH_B, HEAD_DIM), st(logf_s, dec, 1, H_B))
```

```python
import functools

import numpy as np
import jax
import jax.numpy as jnp
from jax import lax
from jax.experimental import pallas as pl
from jax.experimental.pallas import tpu as pltpu

D_MODEL = 4096
BATCH = 4
SEQ = 2048
DEC_BATCH = 32
PAST_LEN = 8192
PAGE_SIZE = 128
HEAD_DIM = 128
H_A = 12
KVH_A = 4
H_IDX = 16
D_IDX = 64
TOPK_MAX = 256
H_B = 12
KVH_B = 4
H_C = 4
HD_C = 256
N_MEM = 256
ROPE_THETA = 500000.0
ROT_FRAC = 4
EPS = 1e-6

F32 = jnp.float32
BF16 = jnp.bfloat16
NEG = -1e30
LANES = 128
HEAD_ROWS = 16
MIB = 1024 * 1024

W_A = H_A * HEAD_DIM
W_B = H_B * HEAD_DIM
W_C = H_C * HD_C
W_KV = KVH_A * HEAD_DIM
COL_TILE = 512
LF_LANE = D_IDX + H_IDX


def _layout():
    src, acc = {}, 0
    for n, s in zip(("qa", "ka", "va", "iq", "ik", "iw", "za", "qb", "kb", "vb", "fb", "zb", "qc", "zc", "gates"),
                    (W_A, W_KV, W_KV, H_IDX * D_IDX, D_IDX, H_IDX, W_A, W_B, W_KV, W_KV, H_B, W_B, W_C, W_C,
                     3 * D_MODEL)):
        src[n] = acc
        acc += s
    names = ("qa", "qb", "za", "zb", "ka", "va", "kb", "vb", "iq", "qc", "zc", "gates", "s1", "s2")
    sizes = (W_A, W_B, W_A, W_B, W_KV, W_KV, W_KV, W_KV, H_IDX * D_IDX, W_C, W_C, 3 * D_MODEL, COL_TILE, COL_TILE)
    starts = dict(src, s1=src["ik"], s2=src["fb"])
    off, tiles, acc = {}, [], 0
    for n, s in zip(names, sizes):
        off[n] = acc
        acc += s
        tiles += [starts[n] + COL_TILE * i for i in range(s // COL_TILE)]
    return off, acc, tiles


def _params(n_axes, vmem_mib):
    return pltpu.CompilerParams(dimension_semantics=("arbitrary",) * n_axes,
                                vmem_limit_bytes=vmem_mib * MIB)


def _sigmoid(z):
    return 1.0 / (1.0 + jnp.exp(-z))


def _silu(z):
    return z * _sigmoid(z)


def _log_sigmoid(z):
    return -(jnp.maximum(-z, 0.0) + jnp.log1p(jnp.exp(-jnp.abs(z))))


def _rot(x, c, sa, sb, half):
    n = x.shape[-1]
    return x * c + pltpu.roll(x, n - half, 1) * sa + pltpu.roll(x, half, 1) * sb


def _dot_nt(a, b):
    return lax.dot_general(a, b, (((1,), (1,)), ((), ())), preferred_element_type=F32)


def _split3(x):
    hi = x.astype(BF16)
    r = x - hi.astype(F32)
    mid = r.astype(BF16)
    lo = (r - mid.astype(F32)).astype(BF16)
    return hi, mid, lo


KEY_NEG_INF = -2139095041
KEY_POS_INF = 2139095040


def _key_to_float(key):
    key = jnp.clip(key, KEY_NEG_INF, KEY_POS_INF)
    return lax.bitcast_convert_type(jnp.where(key < 0, key ^ jnp.int32(0x7FFFFFFF), key), F32)


def _kth_largest(count_ge, rows, k):
    kf = jnp.float32(k)
    zero = jnp.zeros((rows, 1), jnp.int32)
    thr = jnp.where(count_ge(_key_to_float(zero)) >= kf, zero, jnp.int32(-2 ** 31))

    def body(i, thr):
        cand = thr + jnp.left_shift(jnp.int32(1), jnp.int32(30) - i)
        return jnp.where(count_ge(_key_to_float(cand)) >= kf, cand, thr)

    return _key_to_float(lax.fori_loop(0, 31, body, thr))


def _rmsnorm_body(x_ref, g_ref, o_ref):
    x = x_ref[...]
    ms = jnp.mean(x * x, axis=-1, keepdims=True)
    o_ref[...] = ((x * lax.rsqrt(ms + EPS)) * g_ref[...]).astype(o_ref.dtype)


def _rmsnorm(x, g, out_dtype, bm):
    m, d = x.shape
    bm = min(bm, m)
    return pl.pallas_call(
        _rmsnorm_body,
        grid=(m // bm,),
        in_specs=[pl.BlockSpec((bm, d), lambda i: (i, 0)), pl.BlockSpec((1, d), lambda i: (0, 0))],
        out_specs=pl.BlockSpec((bm, d), lambda i: (i, 0)),
        out_shape=jax.ShapeDtypeStruct((m, d), out_dtype),
        compiler_params=_params(1, 40),
        name="rmsnorm",
    )(x, g.reshape(1, d))


def _mm_body(x_ref, w_ref, o_ref):
    o_ref[...] = jnp.dot(x_ref[...], w_ref[...], preferred_element_type=F32)


def _mm_nt_body(x_ref, wt_ref, o_ref):
    o_ref[...] = _dot_nt(x_ref[...], wt_ref[...])


def _matmul_nt(x, wt, bm, bn, name):
    m, k = x.shape
    n = wt.shape[0]
    bm, bn = min(bm, m), min(bn, n)
    if n % bn:
        bn = COL_TILE
    assert m % bm == 0 and n % bn == 0
    return pl.pallas_call(
        _mm_nt_body,
        grid=(m // bm, n // bn),
        in_specs=[pl.BlockSpec((bm, k), lambda i, j: (i, 0)), pl.BlockSpec((bn, k), lambda i, j: (j, 0))],
        out_specs=pl.BlockSpec((bm, bn), lambda i, j: (i, j)),
        out_shape=jax.ShapeDtypeStruct((m, n), F32),
        compiler_params=_params(2, 52),
        name=name,
    )(x, wt)


def _matmul(x, w, bm, bn, name):
    m, k = x.shape
    n = w.shape[1]
    bm, bn = min(bm, m), min(bn, n)
    return pl.pallas_call(
        _mm_body,
        grid=(m // bm, n // bn),
        in_specs=[pl.BlockSpec((bm, k), lambda i, j: (i, 0)), pl.BlockSpec((k, bn), lambda i, j: (0, j))],
        out_specs=pl.BlockSpec((bm, bn), lambda i, j: (i, j)),
        out_shape=jax.ShapeDtypeStruct((m, n), F32),
        compiler_params=_params(2, 48),
        name=name,
    )(x, w)


def _wprep_body(shifts, tab_ref, a_ref, b_ref, o_ref):
    shift = tab_ref[1, pl.program_id(1)]

    @pl.when(shift == 0)
    def _():
        o_ref[...] = a_ref[...].astype(o_ref.dtype)

    for sh in shifts:
        @pl.when(shift == sh)
        def _(sh=sh):
            x = jnp.concatenate([a_ref[...], b_ref[...]], axis=0).astype(F32)
            o_ref[...] = pltpu.roll(x, 2 * COL_TILE - sh, 0)[:COL_TILE, :].astype(o_ref.dtype)


def _wprep(wt, tiles, bk):
    n_src, k = wt.shape
    last = pl.cdiv(n_src, COL_TILE) - 1
    tab = jnp.asarray([[c // COL_TILE for c in tiles], [c % COL_TILE for c in tiles]], jnp.int32)
    shifts = tuple(sorted({c % COL_TILE for c in tiles} - {0}))
    bk = min(bk, k)
    grid_spec = pltpu.PrefetchScalarGridSpec(
        num_scalar_prefetch=1,
        grid=(k // bk, len(tiles)),
        in_specs=[pl.BlockSpec((COL_TILE, bk), lambda i, t, tab: (tab[0, t], i)),
                  pl.BlockSpec((COL_TILE, bk), lambda i, t, tab: (jnp.minimum(tab[0, t] + 1, last), i))],
        out_specs=pl.BlockSpec((COL_TILE, bk), lambda i, t, tab: (t, i)),
    )
    return pl.pallas_call(
        functools.partial(_wprep_body, shifts),
        grid_spec=grid_spec,
        out_shape=jax.ShapeDtypeStruct((COL_TILE * len(tiles), k), BF16),
        compiler_params=_params(2, 40),
        name="w_prep",
    )(tab, wt, wt)


def _cast_body(x_ref, o_ref):
    o_ref[...] = x_ref[...].astype(o_ref.dtype)


def _cast_bf16(w, bm):
    m, n = w.shape
    return pl.pallas_call(
        _cast_body,
        grid=(m // bm,),
        in_specs=[pl.BlockSpec((bm, n), lambda i: (i, 0))],
        out_specs=pl.BlockSpec((bm, n), lambda i: (i, 0)),
        out_shape=jax.ShapeDtypeStruct((m, n), BF16),
        compiler_params=_params(1, 40),
        name="w_cast",
    )(w)


def _post_body(ka_ref, s1_ref, s2_ref, tab_ref, bf_ref, ka_o, sm_o):
    c, sa, sb = tab_ref[0], tab_ref[1], tab_ref[2]
    for h in range(KVH_A):
        sl = slice(h * HEAD_DIM, (h + 1) * HEAD_DIM)
        ka_o[:, sl] = _rot(ka_ref[:, sl], c, sa, sb, HEAD_DIM // ROT_FRAC // 2)
    r = _rot(s1_ref[...], tab_ref[6], tab_ref[7], tab_ref[8], D_IDX // ROT_FRAC // 2)
    lf = _log_sigmoid(pltpu.roll(s2_ref[...], LF_LANE, 1) + bf_ref[...])
    lane = lax.broadcasted_iota(jnp.int32, r.shape, 1)
    sm_o[...] = jnp.where(lane < LF_LANE, r, jnp.where(lane < LF_LANE + H_B, lf, 0.0))


def _post(proj, tab, bf_row, seq_rows, bm):
    off = _layout()[0]
    m = proj.shape[0]
    bm = min(bm, m)
    tb = seq_rows // bm
    return pl.pallas_call(
        _post_body,
        grid=(m // bm,),
        in_specs=[pl.BlockSpec((bm, W_KV), lambda i: (i, off["ka"] // W_KV)),
                  pl.BlockSpec((bm, LANES), lambda i: (i, off["s1"] // LANES)),
                  pl.BlockSpec((bm, LANES), lambda i: (i, off["s2"] // LANES)),
                  pl.BlockSpec((9, bm, LANES), lambda i: (0, i % tb, 0)),
                  pl.BlockSpec((1, LANES), lambda i: (0, 0))],
        out_specs=[pl.BlockSpec((bm, W_KV), lambda i: (i, 0)), pl.BlockSpec((bm, LANES), lambda i: (i, 0))],
        out_shape=[jax.ShapeDtypeStruct((m, W_KV), F32), jax.ShapeDtypeStruct((m, LANES), F32)],
        compiler_params=_params(1, 40),
        name="kv_post",
    )(proj, proj, proj, tab, bf_row)


def _qrot_body(qa_ref, iq_ref, tab_ref, qa_o, iq_o):
    for h in range(H_A):
        sl = slice(h * HEAD_DIM, (h + 1) * HEAD_DIM)
        qa_o[:, sl] = _rot(qa_ref[:, sl], tab_ref[0], tab_ref[1], tab_ref[2], HEAD_DIM // ROT_FRAC // 2)
    for hp in range(H_IDX * D_IDX // LANES):
        sl = slice(hp * LANES, (hp + 1) * LANES)
        iq_o[:, sl] = _rot(iq_ref[:, sl], tab_ref[3], tab_ref[4], tab_ref[5], D_IDX // ROT_FRAC // 2)


def _qrot(proj, tab):
    off = _layout()[0]
    m = proj.shape[0]
    wi = H_IDX * D_IDX
    return pl.pallas_call(
        _qrot_body,
        grid=(1,),
        in_specs=[pl.BlockSpec((m, W_A), lambda i: (0, off["qa"] // W_A)),
                  pl.BlockSpec((m, wi), lambda i: (0, off["iq"] // wi)),
                  pl.BlockSpec((9, m, LANES), lambda i: (0, 0, 0))],
        out_specs=[pl.BlockSpec((m, W_A), lambda i: (0, 0)), pl.BlockSpec((m, wi), lambda i: (0, 0))],
        out_shape=[jax.ShapeDtypeStruct((m, W_A), F32), jax.ShapeDtypeStruct((m, wi), F32)],
        compiler_params=_params(1, 32),
        name="q_rotary",
    )(proj, proj, tab)


def _cumsum_body(sm_ref, ccol_ref, crow_ref):
    r = lax.broadcasted_iota(jnp.int32, (LANES, LANES), 0)
    c = lax.broadcasted_iota(jnp.int32, (LANES, LANES), 1)
    tri = (c <= r).astype(BF16)
    keep = (c >= LF_LANE) & (c < LF_LANE + H_B)
    carry = jnp.zeros((1, LANES), F32)
    for blk in range(ccol_ref.shape[0] // LANES):
        rows = slice(blk * LANES, (blk + 1) * LANES)
        x = jnp.where(keep, sm_ref[rows, :], 0.0)
        hi, mid, lo = _split3(x)
        cb = (jnp.dot(tri, hi, preferred_element_type=F32) + jnp.dot(tri, mid, preferred_element_type=F32)
              + jnp.dot(tri, lo, preferred_element_type=F32)) + carry
        carry = cb[LANES - 1:LANES, :]
        ccol_ref[rows, :] = cb
        per = crow_ref.shape[3] // LANES
        crow_ref[0, blk // per, :, (blk % per) * LANES:(blk % per + 1) * LANES] = cb.T


def _cumsum(small, batch, seq, tk):
    return pl.pallas_call(
        _cumsum_body,
        grid=(batch,),
        in_specs=[pl.BlockSpec((seq, LANES), lambda b: (b, 0))],
        out_specs=[pl.BlockSpec((seq, LANES), lambda b: (b, 0)),
                   pl.BlockSpec((1, seq // tk, LANES, tk), lambda b: (b, 0, 0, 0))],
        out_shape=[jax.ShapeDtypeStruct((batch * seq, LANES), F32),
                   jax.ShapeDtypeStruct((batch, seq // tk, LANES, tk), F32)],
        compiler_params=_params(1, 32),
        name="fox_cumsum",
    )(small)


def _softmax_av(s, v):
    m = jnp.max(s, axis=1, keepdims=True)
    p = jnp.exp(s - m)
    l = jnp.sum(p, axis=1, keepdims=True)
    return jnp.dot(p.astype(BF16), v, preferred_element_type=F32) / l


def _lane_pieces(x):
    return [x[:, t * LANES:(t + 1) * LANES] for t in range(x.shape[1] // LANES)]


def _lane_tile(x, width):
    return jnp.concatenate([x] * (width // LANES), axis=1)


def _attend_two_pass(n_heads, hd, qi, logits, values, last_mask, z_ref, u_ref, mx_ref, ls_ref, acc_ref):
    tq = tk = mx_ref.shape[1]

    def each_chunk(step):
        if last_mask is None:
            lax.fori_loop(0, qi + 1, lambda ci, c: (step(ci, None), c)[1], 0)
        else:
            lax.fori_loop(0, qi, lambda ci, c: (step(ci, None), c)[1], 0)
            step(qi, last_mask)

    def masked(ci, h, mask):
        s = logits(ci, h)
        return s if mask is None else s + mask

    mx_ref[...] = jnp.full(mx_ref.shape, NEG, F32)

    def row_max(ci, mask):
        for h in range(n_heads):
            m = mx_ref[h]
            for piece in _lane_pieces(masked(ci, h, mask)):
                m = jnp.maximum(m, piece)
            mx_ref[h] = m

    each_chunk(row_max)
    for h in range(n_heads):
        mx_ref[h] = jnp.broadcast_to(jnp.max(mx_ref[h], axis=1, keepdims=True), (tq, LANES))
    ls_ref[...] = jnp.zeros(ls_ref.shape, F32)
    acc_ref[...] = jnp.zeros(acc_ref.shape, F32)

    def accumulate(ci, mask):
        for h in range(n_heads):
            sl = slice(h * hd, (h + 1) * hd)
            p = jnp.exp(masked(ci, h, mask) - _lane_tile(mx_ref[h], tk))
            l = ls_ref[h]
            for piece in _lane_pieces(p):
                l = l + piece
            ls_ref[h] = l
            acc_ref[:, sl] += jnp.dot(p.astype(BF16), values(ci, h), preferred_element_type=F32)

    each_chunk(accumulate)
    for h in range(n_heads):
        sl = slice(h * hd, (h + 1) * hd)
        l = jnp.sum(ls_ref[h], axis=1, keepdims=True)
        u_ref[:, sl] = ((acc_ref[:, sl] / l) * _silu(z_ref[:, sl])).astype(u_ref.dtype)


def _chunk_rows(c, tk):
    return pl.ds(pl.multiple_of(c * tk, tk), tk)


def _dsa_prompt_body(topk, q_ref, iq_ref, sm_ref, z_ref, k_ref, v_ref, ik_ref, tab_ref, u_ref,
                     xb_ref, w_ref, sc_ref, bias_ref, qb_ref, mx_ref, ls_ref, acc_ref):
    tq = q_ref.shape[0]
    qi = pl.program_id(1)
    r = lax.broadcasted_iota(jnp.int32, (tq, tq), 0)
    c = lax.broadcasted_iota(jnp.int32, (tq, tq), 1)
    tri = c <= r

    for hp in range(H_IDX * D_IDX // LANES):
        sl = slice(hp * LANES, (hp + 1) * LANES)
        xb_ref[:, sl] = _rot(iq_ref[:, sl], tab_ref[3], tab_ref[4], tab_ref[5], D_IDX // ROT_FRAC // 2).astype(BF16)
    for h in range(H_IDX):
        w_ref[h] = jnp.broadcast_to(sm_ref[:, D_IDX + h:D_IDX + h + 1] * (H_IDX ** -0.5), (tq, LANES))

    def scores(ci):
        rows = _chunk_rows(ci, tq)
        acc = jnp.zeros((tq, tq), F32)
        for hp in range(H_IDX * D_IDX // LANES):
            xb = xb_ref[:, hp * LANES:(hp + 1) * LANES]
            for j in range(LANES // D_IDX):
                h = hp * (LANES // D_IDX) + j
                s = _dot_nt(xb, ik_ref[rows, j * LANES:(j + 1) * LANES]) * (D_IDX ** -0.5)
                acc = acc + jnp.maximum(s, 0.0) * _lane_tile(w_ref[h], tq)
        return acc

    def fill_scores(ci, carry):
        sc_ref[ci] = scores(ci)
        return carry

    lax.fori_loop(0, qi, fill_scores, 0)
    sc_ref[qi] = jnp.where(tri, scores(qi), -jnp.inf)

    def count(hit):
        def add(ci, cnt):
            for piece in _lane_pieces(jnp.where(hit(sc_ref[ci]), 1.0, 0.0)):
                cnt = cnt + piece
            return cnt
        cnt = lax.fori_loop(0, qi + 1, add, jnp.zeros((tq, LANES), F32))
        return jnp.sum(cnt, axis=1, keepdims=True)

    thr = _kth_largest(lambda t: count(lambda x: x >= t), tq, topk)

    need = jnp.float32(topk) - count(lambda x: x > thr)
    prefix = (r <= c).astype(BF16)

    def keep_bias(ci, run, mask):
        kk = sc_ref[ci]
        tie = kk == thr
        tie_f = jnp.where(tie, 1.0, 0.0)
        seen = jnp.dot(tie_f.astype(BF16), prefix, preferred_element_type=F32) + run
        keep = (kk > thr) | (tie & (seen <= need))
        if mask is not None:
            keep = keep & mask
        bias_ref[ci] = jnp.where(keep, 0.0, NEG)
        return run + jnp.sum(tie_f, axis=1, keepdims=True)

    run = lax.fori_loop(0, qi, lambda ci, run: keep_bias(ci, run, None), jnp.zeros((tq, 1), F32))
    keep_bias(qi, run, tri)

    grp = H_A // KVH_A
    for h in range(H_A):
        sl = slice(h * HEAD_DIM, (h + 1) * HEAD_DIM)
        qb_ref[:, sl] = _rot(q_ref[:, sl], tab_ref[0], tab_ref[1], tab_ref[2],
                             HEAD_DIM // ROT_FRAC // 2).astype(BF16)

    def kv_cols(h):
        return slice(h // grp * HEAD_DIM, (h // grp + 1) * HEAD_DIM)

    def logits(ci, h):
        qh = qb_ref[:, h * HEAD_DIM:(h + 1) * HEAD_DIM]
        return _dot_nt(qh, k_ref[_chunk_rows(ci, tq), kv_cols(h)]) * (HEAD_DIM ** -0.5) + bias_ref[ci]

    def values(ci, h):
        return v_ref[_chunk_rows(ci, tq), kv_cols(h)]

    _attend_two_pass(H_A, HEAD_DIM, qi, logits, values, None, z_ref, u_ref, mx_ref, ls_ref, acc_ref)


def _dsa_prompt(proj, small, ka_bf, va_bf, ik_bf, tab, batch, seq, tq):
    off = _layout()[0]
    nq = seq // tq
    wi = H_IDX * D_IDX
    topk = min(TOPK_MAX, seq // 4)
    row = lambda b, q: b * nq + q
    return pl.pallas_call(
        functools.partial(_dsa_prompt_body, topk),
        grid=(batch, nq),
        scratch_shapes=[pltpu.VMEM((tq, wi), BF16), pltpu.VMEM((H_IDX, tq, LANES), F32),
                        pltpu.VMEM((nq, tq, tq), F32), pltpu.VMEM((nq, tq, tq), F32),
                        pltpu.VMEM((tq, W_A), BF16), pltpu.VMEM((H_A, tq, LANES), F32),
                        pltpu.VMEM((H_A, tq, LANES), F32), pltpu.VMEM((tq, W_A), F32)],
        in_specs=[pl.BlockSpec((tq, W_A), lambda b, q: (row(b, q), off["qa"] // W_A)),
                  pl.BlockSpec((tq, wi), lambda b, q: (row(b, q), off["iq"] // wi)),
                  pl.BlockSpec((tq, LANES), lambda b, q: (row(b, q), 0)),
                  pl.BlockSpec((tq, W_A), lambda b, q: (row(b, q), off["za"] // W_A)),
                  pl.BlockSpec((seq, W_KV), lambda b, q: (b, 0)),
                  pl.BlockSpec((seq, W_KV), lambda b, q: (b, 0)),
                  pl.BlockSpec((seq, 2 * LANES), lambda b, q: (b, 0)),
                  pl.BlockSpec((9, tq, LANES), lambda b, q: (0, q, 0))],
        out_specs=pl.BlockSpec((tq, W_A), lambda b, q: (row(b, q), 0)),
        out_shape=jax.ShapeDtypeStruct((batch * seq, W_A), BF16),
        compiler_params=_params(2, 56),
        name="dsa_prompt",
    )(proj, proj, small, proj, ka_bf, va_bf, ik_bf, tab)


def _fox_prompt_body(q_ref, z_ref, k_ref, v_ref, ccol_ref, crow_ref, u_ref, qb_ref, cq_ref, mx_ref, ls_ref, acc_ref):
    tq = q_ref.shape[0]
    qi = pl.program_id(1)
    r = lax.broadcasted_iota(jnp.int32, (tq, tq), 0)
    c = lax.broadcasted_iota(jnp.int32, (tq, tq), 1)
    diag_bias = jnp.where(c <= r, 0.0, NEG)
    grp = H_B // KVH_B
    qb_ref[...] = q_ref[...].astype(BF16)
    for h in range(H_B):
        cq_ref[h] = jnp.broadcast_to(ccol_ref[:, LF_LANE + h:LF_LANE + h + 1], (tq, LANES))

    def kv_cols(h):
        return slice(h // grp * HEAD_DIM, (h // grp + 1) * HEAD_DIM)

    def logits(ci, h):
        qh = qb_ref[:, h * HEAD_DIM:(h + 1) * HEAD_DIM]
        s = _dot_nt(qh, k_ref[_chunk_rows(ci, tq), kv_cols(h)]) * (HEAD_DIM ** -0.5)
        return s + _lane_tile(cq_ref[h], tq) - crow_ref[0, ci, LF_LANE + h:LF_LANE + h + 1, :]

    def values(ci, h):
        return v_ref[_chunk_rows(ci, tq), kv_cols(h)]

    _attend_two_pass(H_B, HEAD_DIM, qi, logits, values, diag_bias, z_ref, u_ref, mx_ref, ls_ref, acc_ref)


def _fox_prompt(proj, kb_bf, vb_bf, ccol, crow, batch, seq, tq):
    off = _layout()[0]
    nq = seq // tq
    row = lambda b, q: b * nq + q
    return pl.pallas_call(
        _fox_prompt_body,
        grid=(batch, nq),
        in_specs=[pl.BlockSpec((tq, W_B), lambda b, q: (row(b, q), off["qb"] // W_B)),
                  pl.BlockSpec((tq, W_B), lambda b, q: (row(b, q), off["zb"] // W_B)),
                  pl.BlockSpec((seq, W_KV), lambda b, q: (b, 0)),
                  pl.BlockSpec((seq, W_KV), lambda b, q: (b, 0)),
                  pl.BlockSpec((tq, LANES), lambda b, q: (row(b, q), 0)),
                  pl.BlockSpec((1, nq, LANES, tq), lambda b, q: (b, 0, 0, 0))],
        out_specs=pl.BlockSpec((tq, W_B), lambda b, q: (row(b, q), 0)),
        scratch_shapes=[pltpu.VMEM((tq, W_B), BF16), pltpu.VMEM((H_B, tq, LANES), F32),
                        pltpu.VMEM((H_B, tq, LANES), F32), pltpu.VMEM((H_B, tq, LANES), F32),
                        pltpu.VMEM((tq, W_B), F32)],
        out_shape=jax.ShapeDtypeStruct((batch * seq, W_B), BF16),
        compiler_params=_params(2, 56),
        name="fox_prompt",
    )(proj, proj, kb_bf, vb_bf, ccol, crow)


def _cross_prompt_body(q_ref, z_ref, k_ref, v_ref, u_ref):
    for h in range(H_C):
        sl = slice(h * HD_C, (h + 1) * HD_C)
        s = _dot_nt(q_ref[:, sl].astype(BF16), k_ref[:, sl]) * (HD_C ** -0.5)
        o = _softmax_av(s, v_ref[:, sl])
        u_ref[:, sl] = (o * _silu(z_ref[:, sl])).astype(u_ref.dtype)


def _cross_prompt(proj, mk_bf, mv_bf, batch, seq, tq):
    off = _layout()[0]
    nq = seq // tq
    row = lambda b, q: b * nq + q
    return pl.pallas_call(
        _cross_prompt_body,
        grid=(batch, nq),
        in_specs=[pl.BlockSpec((tq, W_C), lambda b, q: (row(b, q), off["qc"] // W_C)),
                  pl.BlockSpec((tq, W_C), lambda b, q: (row(b, q), off["zc"] // W_C)),
                  pl.BlockSpec((N_MEM, W_C), lambda b, q: (b, 0)),
                  pl.BlockSpec((N_MEM, W_C), lambda b, q: (b, 0))],
        out_specs=pl.BlockSpec((tq, W_C), lambda b, q: (row(b, q), 0)),
        out_shape=jax.ShapeDtypeStruct((batch * seq, W_C), BF16),
        compiler_params=_params(2, 40),
        name="cross_prompt",
    )(proj, proj, mk_bf, mv_bf)


def _merge_body(ua_ref, ub_ref, uc_ref, wa_ref, wb_ref, wc_ref, ga_ref, gb_ref, gc_ref, h_ref):
    h = (_sigmoid(ga_ref[...]) * jnp.dot(ua_ref[...], wa_ref[...], preferred_element_type=F32)
         + _sigmoid(gb_ref[...]) * jnp.dot(ub_ref[...], wb_ref[...], preferred_element_type=F32)
         + _sigmoid(gc_ref[...]) * jnp.dot(uc_ref[...], wc_ref[...], preferred_element_type=F32))
    h_ref[...] = h.astype(h_ref.dtype)


def _merge(ua, ub, uc, wa, wb, wc, proj, bm, bn):
    off = _layout()[0]
    m = ua.shape[0]
    d = wa.shape[1]
    bm, bn = min(bm, m), min(bn, d)
    g0 = off["gates"] // bn
    nd = d // bn
    return pl.pallas_call(
        _merge_body,
        grid=(m // bm, nd),
        in_specs=[pl.BlockSpec((bm, W_A), lambda i, j: (i, 0)),
                  pl.BlockSpec((bm, W_B), lambda i, j: (i, 0)),
                  pl.BlockSpec((bm, W_C), lambda i, j: (i, 0)),
                  pl.BlockSpec((W_A, bn), lambda i, j: (0, j)),
                  pl.BlockSpec((W_B, bn), lambda i, j: (0, j)),
                  pl.BlockSpec((W_C, bn), lambda i, j: (0, j)),
                  pl.BlockSpec((bm, bn), lambda i, j: (i, g0 + j)),
                  pl.BlockSpec((bm, bn), lambda i, j: (i, g0 + nd + j)),
                  pl.BlockSpec((bm, bn), lambda i, j: (i, g0 + 2 * nd + j))],
        out_specs=pl.BlockSpec((bm, bn), lambda i, j: (i, j)),
        out_shape=jax.ShapeDtypeStruct((m, d), BF16),
        compiler_params=_params(2, 48),
        name="merge",
    )(ua, ub, uc, wa, wb, wc, proj, proj, proj)


def _outproj_body(x_ref, h_ref, w_ref, g_ref, o_ref, y_ref):
    j = pl.program_id(1)
    nj = pl.num_programs(1)
    bn = w_ref.shape[1]
    y_ref[j] = x_ref[...] + jnp.dot(h_ref[...], w_ref[...], preferred_element_type=F32)

    @pl.when(j == nj - 1)
    def _():
        n_tiles = y_ref.shape[0]
        ss = jnp.zeros((y_ref.shape[1], 1), F32)
        for t in range(n_tiles):
            y = y_ref[t]
            ss = ss + jnp.sum(y * y, axis=-1, keepdims=True)
        inv = lax.rsqrt(ss * (1.0 / (n_tiles * bn)) + EPS)
        for t in range(n_tiles):
            o_ref[:, t * bn:(t + 1) * bn] = (y_ref[t] * inv) * g_ref[:, t * bn:(t + 1) * bn]


def _outproj_norm(x, h, w, g, bm, bn):
    m, d = x.shape
    bm, bn = min(bm, m), min(bn, d)
    return pl.pallas_call(
        _outproj_body,
        grid=(m // bm, d // bn),
        in_specs=[pl.BlockSpec((bm, bn), lambda i, j: (i, j)),
                  pl.BlockSpec((bm, d), lambda i, j: (i, 0)),
                  pl.BlockSpec((d, bn), lambda i, j: (0, j)),
                  pl.BlockSpec((1, d), lambda i, j: (0, 0))],
        out_specs=pl.BlockSpec((bm, d), lambda i, j: (i, 0)),
        out_shape=jax.ShapeDtypeStruct((m, d), F32),
        scratch_shapes=[pltpu.VMEM((d // bn, bm, bn), F32)],
        compiler_params=_params(2, 52),
        name="out_proj",
    )(x, h, w, g.reshape(1, d))


def _idx_sample_body(pps, pt_ref, iq_ref, w_ref, ikn_ref, *rest):
    pages = rest[:pps]
    sc_ref, own_ref = rest[pps], rest[pps + 1]
    iq = iq_ref[0]
    w = w_ref[0] * (H_IDX ** -0.5)
    for i in range(pps):
        s = jnp.dot(iq, pages[i][...].astype(BF16), preferred_element_type=F32) * (D_IDX ** -0.5)
        sc_ref[0, :, i * PAGE_SIZE:(i + 1) * PAGE_SIZE] = jnp.sum(jnp.maximum(s, 0.0) * w, axis=0, keepdims=True)

    @pl.when(pl.program_id(1) == 0)
    def _():
        kn = ikn_ref[0].astype(F32)
        s = jnp.sum(iq.astype(F32) * kn, axis=1, keepdims=True) * (D_IDX ** -0.5)
        own = jnp.sum(jnp.maximum(s, 0.0) * w, axis=0, keepdims=True)
        lane = lax.broadcasted_iota(jnp.int32, (1, LANES), 1)
        own_ref[0] = jnp.where(lane == 0, own, -jnp.inf)


def _idx_sample(page_table, iq_bf, iw, ik_new_bf, pool_idx, pps):
    b, n_pages = page_table.shape
    npc = n_pages // pps
    page_specs = [pl.BlockSpec((None, None, D_IDX, PAGE_SIZE),
                               lambda bi, pc, pt, i=i: (0, pt[bi, pc * pps + i], 0, 0)) for i in range(pps)]
    grid_spec = pltpu.PrefetchScalarGridSpec(
        num_scalar_prefetch=1,
        grid=(b, npc),
        in_specs=[pl.BlockSpec((1, H_IDX, D_IDX), lambda bi, pc, pt: (bi, 0, 0)),
                  pl.BlockSpec((1, H_IDX, 1), lambda bi, pc, pt: (bi, 0, 0)),
                  pl.BlockSpec((1, 1, D_IDX), lambda bi, pc, pt: (bi, 0, 0))] + page_specs,
        out_specs=[pl.BlockSpec((1, 1, pps * PAGE_SIZE), lambda bi, pc, pt: (bi, 0, pc)),
                   pl.BlockSpec((1, 1, LANES), lambda bi, pc, pt: (bi, 0, 0))],
    )
    return pl.pallas_call(
        functools.partial(_idx_sample_body, pps),
        grid_spec=grid_spec,
        out_shape=[jax.ShapeDtypeStruct((b, 1, n_pages * PAGE_SIZE), F32),
                   jax.ShapeDtypeStruct((b, 1, LANES), F32)],
        compiler_params=_params(2, 32),
        name="idx_sample",
    )(page_table, iq_bf, iw, ik_new_bf, *([pool_idx] * pps))


def _topk_sample_body(topk, n_valid, sc_ref, pos_ref, rank_ref, sq_ref, acc_ref):
    sc = sc_ref[...]
    nb = sc.shape[0]
    lane = lax.broadcasted_iota(jnp.int32, sc.shape, 1)
    valid = lane < n_valid
    key = jnp.where(valid, sc, -jnp.inf)

    def count(hit):
        return jnp.sum(jnp.where(hit, 1.0, 0.0), axis=1, keepdims=True)

    thr = _kth_largest(lambda t: count(key >= t), nb, topk)
    need = jnp.float32(topk) - count(key > thr)
    pr = lax.broadcasted_iota(jnp.int32, (LANES, LANES), 0)
    pc = lax.broadcasted_iota(jnp.int32, (LANES, LANES), 1)
    prefix = (pr <= pc).astype(BF16)
    ties = jnp.zeros((nb, 1), F32)
    kept = jnp.zeros((nb, 1), F32)
    n_chunks = sc.shape[1] // LANES
    for c in range(n_chunks):
        kk = key[:, c * LANES:(c + 1) * LANES]
        tie = kk == thr
        tie_f = jnp.where(tie, 1.0, 0.0)
        seen = jnp.dot(tie_f.astype(BF16), prefix, preferred_element_type=F32) + ties
        in_range = lax.broadcasted_iota(jnp.int32, kk.shape, 1) < n_valid - c * LANES
        keep_f = jnp.where(((kk > thr) | (tie & (seen <= need))) & in_range, 1.0, 0.0)
        order = jnp.dot(keep_f.astype(BF16), prefix, preferred_element_type=F32) + kept
        rank_ref[c] = keep_f * order
        ties = ties + jnp.sum(tie_f, axis=1, keepdims=True)
        kept = kept + jnp.sum(keep_f, axis=1, keepdims=True)

    sq_ref[...] = jnp.zeros(sq_ref.shape, F32)
    acc_ref[...] = jnp.zeros(acc_ref.shape, F32)
    slot_no = (lax.broadcasted_iota(jnp.int32, (LANES, topk), 1) + 1).astype(F32)
    key_no = lax.broadcasted_iota(jnp.int32, (LANES, 1), 0)

    def place(c, carry):
        sq_ref[0:nb, :] = rank_ref[c]
        ranks = sq_ref[...].T
        position = (key_no + c * LANES).astype(F32)
        for b in range(nb):
            hit = ranks[:, b:b + 1] == slot_no
            acc_ref[b:b + 1, :] += jnp.sum(jnp.where(hit, position, 0.0), axis=0, keepdims=True)
        return carry

    lax.fori_loop(0, n_chunks, place, 0)
    pos_ref[...] = acc_ref[...].astype(jnp.int32)


def _topk_sample(scores, n_valid):
    topk = min(TOPK_MAX, n_valid // 4)
    b, n = scores.shape
    return pl.pallas_call(
        functools.partial(_topk_sample_body, topk, n_valid),
        grid=(1,),
        in_specs=[pl.BlockSpec((b, n), lambda i: (0, 0))],
        out_specs=pl.BlockSpec((b, topk), lambda i: (0, 0)),
        out_shape=jax.ShapeDtypeStruct((b, topk), jnp.int32),
        scratch_shapes=[pltpu.VMEM((n // LANES, b, LANES), F32), pltpu.VMEM((LANES, LANES), F32),
                        pltpu.VMEM((b, topk), F32)],
        compiler_params=_params(1, 40),
        name="topk_sample",
    )(scores)


def _dsa_gather_body(scale, grp, n_keys, pos_ref, pt_ref, q_ref, sb_ref, gm_ref, kn_ref, vn_ref, nb_ref, z_ref,
                     k_hbm, v_hbm, u_ref, kbuf, vbuf, sem):
    b = pl.program_id(0)
    n_seq = pl.num_programs(0)
    topk = pos_ref.shape[1]
    slot = b % 2

    def row_copies(seq, j, to_slot):
        p = jnp.minimum(pos_ref[seq, j], n_keys - 1)
        page = pt_ref[seq, lax.shift_right_logical(p, 7)]
        src = pl.ds((p & (PAGE_SIZE - 1)) * grp, grp)
        dst = pl.ds(j * grp, grp)
        return (pltpu.make_async_copy(k_hbm.at[page, src, :], kbuf.at[to_slot, dst, :], sem.at[to_slot]),
                pltpu.make_async_copy(v_hbm.at[page, src, :], vbuf.at[to_slot, dst, :], sem.at[to_slot]))

    def fetch(seq, to_slot):
        def start(j, carry):
            for priority, cp in enumerate(row_copies(seq, j, to_slot)):
                cp.start(priority=priority)
            return carry
        lax.fori_loop(0, topk, start, 0)

    @pl.when(b == 0)
    def _():
        fetch(0, 0)

    @pl.when(b + 1 < n_seq)
    def _():
        fetch(b + 1, 1 - slot)

    def finish(j, carry):
        for cp in row_copies(b, j, slot):
            cp.wait()
        return carry

    lax.fori_loop(0, topk, finish, 0)

    q = q_ref[0]
    s = (_dot_nt(q, kbuf[slot].astype(BF16)) * scale + sb_ref[0]) + gm_ref[...]
    kn = kn_ref[0].astype(BF16).astype(F32)
    vn = vn_ref[0].astype(BF16).astype(F32)
    nb = nb_ref[0]
    on = nb > 0.5 * NEG
    s0 = jnp.where(on, jnp.sum(q.astype(F32) * kn, axis=1, keepdims=True) * scale + nb, NEG)
    m = jnp.maximum(jnp.max(s, axis=1, keepdims=True), s0)
    p = jnp.where(s > 0.5 * NEG, jnp.exp(s - m), 0.0)
    p0 = jnp.where(on, jnp.exp(s0 - m), 0.0)
    l = jnp.sum(p, axis=1, keepdims=True) + p0
    acc = jnp.dot(p.astype(BF16), vbuf[slot].astype(BF16), preferred_element_type=F32) + p0 * vn
    u_ref[0] = (acc / l) * _silu(z_ref[0])


def _dsa_gather(pos, page_table, q, slot_bias, gmask, k_new, v_new, new_bias, z, pool_k, pool_v, scale, grp):
    b, topk = pos.shape
    hd = pool_k.shape[2]
    rows = topk * grp
    n_keys = page_table.shape[1] * PAGE_SIZE
    per_b = lambda bi, pos, pt: (bi, 0, 0)
    head_blk = pl.BlockSpec((1, HEAD_ROWS, hd), per_b)
    grid_spec = pltpu.PrefetchScalarGridSpec(
        num_scalar_prefetch=2,
        grid=(b,),
        in_specs=[head_blk,
                  pl.BlockSpec((1, 1, rows), per_b),
                  pl.BlockSpec((HEAD_ROWS, rows), lambda bi, pos, pt: (0, 0)),
                  head_blk, head_blk,
                  pl.BlockSpec((1, HEAD_ROWS, 1), per_b),
                  head_blk,
                  pl.BlockSpec(memory_space=pl.ANY),
                  pl.BlockSpec(memory_space=pl.ANY)],
        out_specs=head_blk,
        scratch_shapes=[pltpu.VMEM((2, rows, hd), F32), pltpu.VMEM((2, rows, hd), F32),
                        pltpu.SemaphoreType.DMA((2,))],
    )
    return pl.pallas_call(
        functools.partial(_dsa_gather_body, scale, grp, n_keys),
        grid_spec=grid_spec,
        out_shape=jax.ShapeDtypeStruct((b, HEAD_ROWS, hd), F32),
        compiler_params=_params(1, 32),
        name="dsa_sample",
    )(pos, page_table, q, slot_bias, gmask, k_new, v_new, new_bias, z, pool_k, pool_v)


def _fox_bias_body(pps, grp, pt_ref, lfn_ref, *rest):
    pages = rest[:pps]
    d_ref, x_ref, carry_ref = rest[pps], rest[pps + 1], rest[pps + 2]
    rows = PAGE_SIZE * grp
    j = lax.broadcasted_iota(jnp.int32, (LANES, rows), 0)
    r = lax.broadcasted_iota(jnp.int32, (LANES, rows), 1)
    upper = (r < j * grp).astype(BF16)

    @pl.when(pl.program_id(1) == 0)
    def _():
        x_ref[...] = jnp.zeros_like(x_ref)
        x_ref[0, 0:1, :] = lfn_ref[0]
        carry_ref[...] = x_ref[0].T[0:HEAD_ROWS, 0:1]
        x_ref[0, 0:1, :] = jnp.zeros((1, LANES), F32)

    carry = carry_ref[...]
    for i in reversed(range(pps)):
        x_ref[i, :, 0:H_B] = pages[i][...]
        xt = x_ref[i].T[0:HEAD_ROWS, :]
        hi, mid, lo = _split3(xt)
        d_loc = (jnp.dot(hi, upper, preferred_element_type=F32) + jnp.dot(mid, upper, preferred_element_type=F32)
                 + jnp.dot(lo, upper, preferred_element_type=F32))
        d_ref[0, :, i * rows:(i + 1) * rows] = d_loc + carry
        carry = carry + jnp.sum(xt, axis=1, keepdims=True)
    carry_ref[...] = carry


def _fox_bias(page_table, lf_new, pool_lf, pps, grp):
    b, n_pages = page_table.shape
    npc = n_pages // pps
    rows = PAGE_SIZE * grp
    page_specs = [pl.BlockSpec((None, None, PAGE_SIZE, H_B),
                               lambda bi, pc, pt, i=i: (0, pt[bi, (npc - 1 - pc) * pps + i], 0, 0))
                  for i in range(pps)]
    grid_spec = pltpu.PrefetchScalarGridSpec(
        num_scalar_prefetch=1,
        grid=(b, npc),
        in_specs=[pl.BlockSpec((1, 1, LANES), lambda bi, pc, pt: (bi, 0, 0))] + page_specs,
        out_specs=pl.BlockSpec((1, HEAD_ROWS, pps * rows), lambda bi, pc, pt: (bi, 0, npc - 1 - pc)),
        scratch_shapes=[pltpu.VMEM((pps, LANES, LANES), F32), pltpu.VMEM((HEAD_ROWS, 1), F32)],
    )
    return pl.pallas_call(
        functools.partial(_fox_bias_body, pps, grp),
        grid_spec=grid_spec,
        out_shape=jax.ShapeDtypeStruct((b, HEAD_ROWS, n_pages * rows), F32),
        compiler_params=_params(2, 32),
        name="fox_bias_sample",
    )(page_table, lf_new, *([pool_lf] * pps))


def _paged_body(pps, scale, pt_ref, q_ref, bias_ref, gm_ref, kn_ref, vn_ref, nb_ref, z_ref, *rest):
    kps, vps = rest[:pps], rest[pps:2 * pps]
    u_ref, m_ref, l_ref, acc_ref = rest[2 * pps:2 * pps + 4]
    pc = pl.program_id(1)
    rows = kps[0].shape[0]
    q = q_ref[0]

    @pl.when(pc == 0)
    def _():
        kn = kn_ref[0].astype(BF16).astype(F32)
        vn = vn_ref[0].astype(BF16).astype(F32)
        nb = nb_ref[0]
        on = nb > 0.5 * NEG
        s0 = jnp.sum(q.astype(F32) * kn, axis=1, keepdims=True) * scale + nb
        m_ref[...] = jnp.where(on, s0, NEG)
        l_ref[...] = jnp.where(on, 1.0, 0.0)
        acc_ref[...] = jnp.where(on, vn, 0.0)

    ss = [(_dot_nt(q, kps[i][...].astype(BF16)) * scale + bias_ref[0, :, i * rows:(i + 1) * rows]) + gm_ref[...]
          for i in range(pps)]
    m_old = m_ref[...]
    m_new = m_old
    for s in ss:
        m_new = jnp.maximum(m_new, jnp.max(s, axis=1, keepdims=True))
    alpha = jnp.exp(m_old - m_new)
    l_new = alpha * l_ref[...]
    acc = alpha * acc_ref[...]
    for i, s in enumerate(ss):
        pf = jnp.where(s > 0.5 * NEG, jnp.exp(s - m_new), 0.0)
        l_new = l_new + jnp.sum(pf, axis=1, keepdims=True)
        acc = acc + jnp.dot(pf.astype(BF16), vps[i][...].astype(BF16), preferred_element_type=F32)
    l_ref[...] = l_new
    acc_ref[...] = acc
    m_ref[...] = m_new

    @pl.when(pc == pl.num_programs(1) - 1)
    def _():
        u_ref[0] = (acc_ref[...] / l_ref[...]) * _silu(z_ref[0])


def _paged_attn(page_idx, q, bias, gmask, k_new, v_new, new_bias, z, pool_k, pool_v, pps, scale, name, paged=True):
    b, n_pages = page_idx.shape
    rows, hd = (pool_k.shape[1] if paged else pool_k.shape[1] // n_pages), pool_k.shape[2]
    rb = bias.shape[1]
    npc = n_pages // pps
    if paged:
        kspecs = [pl.BlockSpec((None, rows, hd), lambda bi, pc, pt, i=i: (pt[bi, pc * pps + i], 0, 0))
                  for i in range(pps)]
    else:
        kspecs = [pl.BlockSpec((None, rows, hd), lambda bi, pc, pt, i=i: (bi, pc * pps + i, 0)) for i in range(pps)]
    per_b = lambda bi, pc, pt: (bi, 0, 0)
    head_blk = pl.BlockSpec((1, HEAD_ROWS, hd), per_b)
    grid_spec = pltpu.PrefetchScalarGridSpec(
        num_scalar_prefetch=1,
        grid=(b, npc),
        in_specs=[head_blk,
                  pl.BlockSpec((1, rb, pps * rows), lambda bi, pc, pt: (bi, 0, pc)),
                  pl.BlockSpec((HEAD_ROWS, rows), lambda bi, pc, pt: (0, 0)),
                  head_blk, head_blk,
                  pl.BlockSpec((1, HEAD_ROWS, 1), per_b),
                  head_blk] + kspecs + kspecs,
        out_specs=head_blk,
        scratch_shapes=[pltpu.VMEM((HEAD_ROWS, 1), F32), pltpu.VMEM((HEAD_ROWS, 1), F32),
                        pltpu.VMEM((HEAD_ROWS, hd), F32)],
    )
    return pl.pallas_call(
        functools.partial(_paged_body, pps, scale),
        grid_spec=grid_spec,
        out_shape=jax.ShapeDtypeStruct((b, HEAD_ROWS, hd), F32),
        compiler_params=_params(2, 48),
        name=name,
    )(page_idx, q, bias, gmask, k_new, v_new, new_bias, z, *([pool_k] * pps), *([pool_v] * pps))


def _rotary_tables(pos):
    def base(width):
        rot = width // ROT_FRAC
        half = rot // 2
        inv = jnp.power(jnp.float32(ROPE_THETA), -jnp.arange(half, dtype=F32) * (2.0 / rot))
        ang = pos.astype(F32)[:, None] * inv[None, :]
        cos, sin = jnp.cos(ang), jnp.sin(ang)
        n = pos.shape[0]
        ones, zeros = jnp.ones((n, width - rot), F32), jnp.zeros((n, width - rot), F32)
        zh = jnp.zeros((n, half), F32)
        return (jnp.concatenate([cos, cos, ones], 1), jnp.concatenate([-sin, zh, zeros], 1),
                jnp.concatenate([zh, sin, zeros], 1))

    c1, a1, b1 = base(HEAD_DIM)
    c2, a2, b2 = base(D_IDX)
    n = pos.shape[0]
    pad1, pad0 = jnp.ones((n, LANES - D_IDX), F32), jnp.zeros((n, LANES - D_IDX), F32)
    return jnp.stack([c1, a1, b1,
                      jnp.concatenate([c2, c2], 1), jnp.concatenate([a2, a2], 1), jnp.concatenate([b2, b2], 1),
                      jnp.concatenate([c2, pad1], 1), jnp.concatenate([a2, pad0], 1), jnp.concatenate([b2, pad0], 1)])


def _head_rows(x, n_heads, hd):
    b = x.shape[0]
    return jnp.pad(x.reshape(b, n_heads, hd), ((0, 0), (0, HEAD_ROWS - n_heads), (0, 0)))


def _group_rows(x, n_heads, n_groups, hd):
    b = x.shape[0]
    gid = np.minimum(np.arange(HEAD_ROWS), n_heads - 1) // (n_heads // n_groups)
    return x.reshape(b, n_groups, hd)[:, gid]


def _group_mask(n_heads, n_groups, n_keys=PAGE_SIZE):
    gid = np.minimum(np.arange(HEAD_ROWS), n_heads - 1) // (n_heads // n_groups)
    r = np.arange(n_keys * n_groups)
    return jnp.asarray(np.where((r[None, :] % n_groups) == gid[:, None], 0.0, NEG), F32)


def _trunk(x2d, proj, ua, ub, uc, w_a, w_b, w_c, w_o, g_final):
    h = _merge(ua, ub, uc, w_a, w_b, w_c, proj, 1024, 512)
    return _outproj_norm(x2d, h, w_o, g_final, 512, 512)


def kernel(x_prompt, x_sample, cache_a_k, cache_a_v, cache_a_idx, cache_b_k, cache_b_v, cache_b_logf, cache_mem_k,
           cache_mem_v, page_table, mem_prompt, g_norm, w_in, b_forget, w_br_a, w_br_b, w_br_c, w_out, g_mem,
           w_mem_kv, g_final):
    off, _, tiles = _layout()
    bsz, seq, d = x_prompt.shape
    dec = x_sample.shape[0]
    tq = min(256, seq)

    w_in_t = _wprep(jnp.transpose(w_in[0].astype(BF16)), tiles, 4096)
    w_a, w_b, w_c, w_o, w_m = (_cast_bf16(w[0], 256) for w in (w_br_a, w_br_b, w_br_c, w_out, w_mem_kv))
    bf_row = jnp.zeros((1, LANES), F32).at[0, LF_LANE:LF_LANE + H_B].set(b_forget[0])

    xp = x_prompt.reshape(bsz * seq, d)
    proj = _matmul_nt(_rmsnorm(xp, g_norm[0], BF16, 256), w_in_t, 1024, 1024, "in_proj")
    tab_p = _rotary_tables(jnp.arange(seq, dtype=jnp.int32))
    ka_rot, small = _post(proj, tab_p, bf_row, seq, 512)
    col = lambda name, width: proj[:, off[name]:off[name] + width]
    va, kb, vb = col("va", W_KV), col("kb", W_KV), col("vb", W_KV)
    ik = small[:, :D_IDX]
    logf = small[:, LF_LANE:LF_LANE + H_B]

    mem_n = _rmsnorm(mem_prompt.reshape(bsz * N_MEM, d), g_mem[0], BF16, 256)
    mkv = _matmul(mem_n, w_m, 1024, 1024, "mem_kv")
    mk, mv = mkv[:, :W_C], mkv[:, W_C:]

    ik_bf = ik.astype(BF16)
    ik_pad = jnp.zeros_like(ik_bf)
    ik2 = jnp.concatenate([ik_bf, ik_pad, ik_pad, ik_bf], axis=1)
    ua = _dsa_prompt(proj, small, ka_rot.astype(BF16), va.astype(BF16), ik2, tab_p, bsz, seq, tq)
    ccol, crow = _cumsum(small, bsz, seq, tq)
    ub = _fox_prompt(proj, kb.astype(BF16), vb.astype(BF16), ccol, crow, bsz, seq, tq)
    uc = _cross_prompt(proj, mk.astype(BF16), mv.astype(BF16), bsz, seq, tq)
    y_prompt = _trunk(xp, proj, ua, ub, uc, w_a, w_b, w_c, w_o, g_final).reshape(bsz, seq, d)

    xs = x_sample.reshape(dec, d)
    proj_s = _matmul_nt(_rmsnorm(xs, g_norm[0], BF16, dec), w_in_t, dec, 1024, "in_proj_sample")
    tab_s = _rotary_tables(jnp.full((dec,), page_table.shape[1] * PAGE_SIZE, jnp.int32))
    ka_s, small_s = _post(proj_s, tab_s, bf_row, dec, dec)
    cols = lambda name, width: proj_s[:, off[name]:off[name] + width]
    va_s, kb_s, vb_s = cols("va", W_KV), cols("kb", W_KV), cols("vb", W_KV)
    ik_s = small_s[:, :D_IDX]
    iw_s = small_s[:, D_IDX:D_IDX + H_IDX]
    logf_s = small_s[:, LF_LANE:LF_LANE + H_B]
    qa_s, iq_s = _qrot(proj_s, tab_s)

    n_pages = page_table.shape[1]
    pps = min(16, n_pages)
    pps_small = min(64, n_pages)
    scores, own = _idx_sample(page_table, iq_s.astype(BF16).reshape(dec, H_IDX, D_IDX), iw_s.reshape(dec, H_IDX, 1),
                              ik_s.astype(BF16).reshape(dec, 1, D_IDX), jnp.transpose(cache_a_idx, (0, 1, 3, 2)),
                              pps_small)
    n_keys = n_pages * PAGE_SIZE
    n_pool = cache_a_k.shape[1]
    flat = lambda pool, g, hd: pool.reshape(n_pool, PAGE_SIZE * g, hd)
    pos = _topk_sample(jnp.concatenate([scores[:, 0], own[:, 0]], axis=1), n_keys + 1)
    is_new = pos == n_keys
    slot_bias = jnp.repeat(jnp.where(is_new, NEG, 0.0).astype(F32), KVH_A, axis=1)[:, None, :]
    new_bias = jnp.where(jnp.any(is_new, axis=1), 0.0, NEG).astype(F32)

    scale = HEAD_DIM ** -0.5
    ua_full = _dsa_gather(
        pos, page_table, _head_rows(qa_s, H_A, HEAD_DIM).astype(BF16), slot_bias,
        _group_mask(H_A, KVH_A, pos.shape[1]), _group_rows(ka_s, H_A, KVH_A, HEAD_DIM),
        _group_rows(va_s, H_A, KVH_A, HEAD_DIM), jnp.broadcast_to(new_bias[:, None, None], (dec, HEAD_ROWS, 1)),
        _head_rows(cols("za", W_A), H_A, HEAD_DIM),
        flat(cache_a_k, KVH_A, HEAD_DIM), flat(cache_a_v, KVH_A, HEAD_DIM), scale, KVH_A)

    lf_new = jnp.pad(logf_s, ((0, 0), (0, LANES - H_B))).reshape(dec, 1, LANES)
    d_bias = _fox_bias(page_table, lf_new, cache_b_logf, min(16, n_pages), KVH_B)
    ub_full = _paged_attn(
        page_table, _head_rows(cols("qb", W_B), H_B, HEAD_DIM).astype(BF16), d_bias,
        _group_mask(H_B, KVH_B), _group_rows(kb_s, H_B, KVH_B, HEAD_DIM), _group_rows(vb_s, H_B, KVH_B, HEAD_DIM),
        jnp.zeros((dec, HEAD_ROWS, 1), F32), _head_rows(cols("zb", W_B), H_B, HEAD_DIM),
        flat(cache_b_k, KVH_B, HEAD_DIM), flat(cache_b_v, KVH_B, HEAD_DIM), pps, scale, "fox_sample")

    mem_pages = N_MEM // PAGE_SIZE
    uc_full = _paged_attn(
        jnp.zeros((dec, mem_pages), jnp.int32), _head_rows(cols("qc", W_C), H_C, HD_C).astype(BF16),
        jnp.zeros((dec, 1, N_MEM * H_C), F32), _group_mask(H_C, H_C),
        jnp.zeros((dec, HEAD_ROWS, HD_C), F32), jnp.zeros((dec, HEAD_ROWS, HD_C), F32),
        jnp.full((dec, HEAD_ROWS, 1), NEG, F32), _head_rows(cols("zc", W_C), H_C, HD_C),
        cache_mem_k.reshape(dec, N_MEM * H_C, HD_C), cache_mem_v.reshape(dec, N_MEM * H_C, HD_C),
        mem_pages, HD_C ** -0.5, "cross_sample", paged=False)

    ua_s = ua_full[:, :H_A].reshape(dec, W_A).astype(BF16)
    ub_s = ub_full[:, :H_B].reshape(dec, W_B).astype(BF16)
    uc_s = uc_full[:, :H_C].reshape(dec, W_C).astype(BF16)
    y_sample = _trunk(xs, proj_s, ua_s, ub_s, uc_s, w_a, w_b, w_c, w_o, g_final).reshape(dec, 1, d)

    st = lambda a, b, t, *tail: a.reshape((1, b, t) + tail)
    return (y_prompt, y_sample,
            st(ka_rot, bsz, seq, KVH_A, HEAD_DIM), st(va, bsz, seq, KVH_A, HEAD_DIM), st(ik, bsz, seq, D_IDX),
            st(kb, bsz, seq, KVH_B, HEAD_DIM), st(vb, bsz, seq, KVH_B, HEAD_DIM), st(logf, bsz, seq, H_B),
            st(mk, bsz, N_MEM, H_C, HD_C), st(mv, bsz, N_MEM, H_C, HD_C),
            st(ka_s, dec, 1, KVH_A, HEAD_DIM), st(va_s, dec, 1, KVH_A, HEAD_DIM), st(ik_s, dec, 1, D_IDX),
            st(kb_s, dec, 1, KVH_B, HEAD_DIM), st(vb_s, dec, 1, KVH_B, HEAD_DIM), st(logf_s, dec, 1, H_B))
```
